```python
import numpy as np
import jax, jax.numpy as jnp
from jax import lax

D_MODEL = 1024
BATCH = 4
SEQ = 8192
DEPTH = 1

GRID_W = 64
MIX_WIDTH = D_MODEL
NA_HEADS = 8
NA_WIDTH = MIX_WIDTH // 2
NA_HEAD_DIM = NA_WIDTH // NA_HEADS
NA_WIN_ROWS = 8
NA_WIN_COLS = 16
HG_HEADS = 4
HG_WIDTH = MIX_WIDTH // 4
HG_VAL_DIM = HG_WIDTH // HG_HEADS
HG_KEY_DIM = 64
HG_KWIDTH = HG_HEADS * HG_KEY_DIM
HG_CHUNK = 64
MEM_TOKENS = 256
MEM_HEADS = 4
MEM_WIDTH = MIX_WIDTH // 4
MEM_HEAD_DIM = MEM_WIDTH // MEM_HEADS
IN_SPLITS = (NA_WIDTH, NA_WIDTH, NA_WIDTH,
             HG_KWIDTH, HG_KWIDTH, HG_KWIDTH,
             HG_WIDTH, HG_WIDTH,
             MEM_WIDTH)
IN_WIDTH = sum(IN_SPLITS)
PEER_HEADS = 8
PEER_NKEYS = 128
PEER_TOPK = 16
PEER_QDIM = 256
N_EXPERTS = PEER_NKEYS * PEER_NKEYS
PEER_BLOCK = 128
RMS_EPS = 1e-6

kernel_name = 'hybrid_natten_hgrn2_memxattn_peer_encoder'


def _rms_normalize(x):
    xf = x.astype(jnp.float32)
    return xf * lax.rsqrt(jnp.mean(xf * xf, axis=-1, keepdims=True) + RMS_EPS)


def rmsnorm(x, g):
    return (_rms_normalize(x) * g.astype(jnp.float32)).astype(x.dtype)


def neighbourhood_attention(q, k, v, rpb):
    B, T, H, Dh = q.shape
    rows = T // GRID_W
    wr = min(NA_WIN_ROWS, rows)
    wc = NA_WIN_COLS
    cols = np.arange(GRID_W)
    c0 = np.clip(cols - wc // 2, 0, GRID_W - wc)
    col_idx = c0[:, None] + np.arange(wc)[None, :]
    dc = col_idx - cols[:, None] + (NA_WIN_COLS - 1)
    bias_c = rpb[:, :, dc]
    qg = q.reshape(B, rows, GRID_W, H, Dh)
    kg = k.reshape(B, rows, GRID_W, H, Dh)
    vg = v.reshape(B, rows, GRID_W, H, Dh)
    scale = Dh ** -0.5

    def one_row(r):
        rs = jnp.clip(r - wr // 2, 0, rows - wr)
        kb = lax.dynamic_slice_in_dim(kg, rs, wr, axis=1)
        vb = lax.dynamic_slice_in_dim(vg, rs, wr, axis=1)
        kw = kb[:, :, col_idx]
        vw = vb[:, :, col_idx]
        qr = lax.dynamic_index_in_dim(qg, r, axis=1, keepdims=False)
        dr = rs + jnp.arange(wr) - r + (NA_WIN_ROWS - 1)
        bias = jnp.take(bias_c, dr, axis=1).transpose(0, 2, 1, 3)
        s = jnp.einsum('bqhd,brqjhd->bhqrj', qr, kw) * scale + bias[None].astype(qr.dtype)
        p = jax.nn.softmax(s.reshape(B, H, GRID_W, wr * wc).astype(jnp.float32), axis=-1)
        p = p.reshape(B, H, GRID_W, wr, wc).astype(v.dtype)
        return jnp.einsum('bhqrj,brqjhd->bqhd', p, vw)

    out = lax.map(one_row, jnp.arange(rows))
    return out.transpose(1, 0, 2, 3, 4).reshape(B, T, H * Dh)


def hgrn2_scan(q, k, v, log_f):
    B, T, H, K = q.shape
    V = v.shape[-1]
    n = T // HG_CHUNK

    def to_chunks(a):
        return a.reshape(B, n, HG_CHUNK, H, a.shape[-1]).transpose(1, 0, 2, 3, 4)

    causal_in_chunk = jnp.tril(jnp.ones((HG_CHUNK, HG_CHUNK), dtype=bool))[None, :, :, None, None]

    def step(S, inp):
        qc, kc, vc, lc = inp
        b = jnp.cumsum(lc, axis=1)
        o_inter = jnp.einsum('bthk,bhkv->bthv', qc * jnp.exp(b), S)
        d = b[:, :, None] - b[:, None]
        decay = jnp.exp(jnp.where(causal_in_chunk, d, -jnp.inf))
        A = jnp.einsum('bthk,bshk,btshk->bhts', qc, kc, decay)
        o = o_inter + jnp.einsum('bhts,bshv->bthv', A, vc)
        b_last = b[:, -1]
        S = jnp.exp(b_last)[..., None] * S + jnp.einsum(
            'bshk,bshv->bhkv', kc * jnp.exp(b_last[:, None] - b), vc)
        return S, o

    S0 = jnp.zeros((B, H, K, V), jnp.float32)
    _, o = lax.scan(step, S0, (to_chunks(q), to_chunks(k), to_chunks(v), to_chunks(log_f)))
    return o.transpose(1, 0, 2, 3, 4).reshape(B, T, H, V)


def hgrn2_bidirectional(q, i, f_pre_fwd, f_pre_bwd, lb_fwd, lb_bwd):
    B, T, _ = q.shape
    qh = q.astype(jnp.float32).reshape(B, T, HG_HEADS, HG_KEY_DIM)
    vh = i.astype(jnp.float32).reshape(B, T, HG_HEADS, HG_VAL_DIM)

    def gates(f_pre, lb):
        f = lb + (1.0 - lb) * jax.nn.sigmoid(f_pre.astype(jnp.float32))
        f = f.reshape(B, T, HG_HEADS, HG_KEY_DIM)
        return 1.0 - f, jnp.log(f)

    kf, lf = gates(f_pre_fwd, lb_fwd)
    kb, lbk = gates(f_pre_bwd, lb_bwd)
    o_fwd = hgrn2_scan(qh, kf, vh, lf)
    flip = lambda a: a[:, ::-1]
    o_bwd = flip(hgrn2_scan(flip(qh), flip(kb), flip(vh), flip(lbk)))
    return o_fwd + o_bwd


def memory_cross_attention(mq, mem_n, w_kv):
    B, T, _ = mq.shape
    M = mem_n.shape[1]
    kv = mem_n @ w_kv
    mk, mv = jnp.split(kv, 2, axis=-1)
    qh = mq.reshape(B, T, MEM_HEADS, MEM_HEAD_DIM)
    mk = mk.reshape(B, M, MEM_HEADS, MEM_HEAD_DIM)
    mv = mv.reshape(B, M, MEM_HEADS, MEM_HEAD_DIM)
    s = jnp.einsum('bthd,bmhd->bhtm', qh, mk) * (MEM_HEAD_DIM ** -0.5)
    p = jax.nn.softmax(s.astype(jnp.float32), axis=-1).astype(mv.dtype)
    return jnp.einsum('bhtm,bmhd->bthd', p, mv).reshape(B, T, MEM_WIDTH)


def peer_ffn(h, w_query, sub_keys, u, v):
    B, T, D = h.shape
    qp = (h @ w_query).reshape(B, T, PEER_HEADS, 2, PEER_QDIM // 2)
    s = jnp.einsum('bthcd,hcnd->bthcn', qp, sub_keys).astype(jnp.float32)
    top_s, top_i = lax.top_k(s, PEER_TOPK)
    cand_s = top_s[..., 0, :, None] + top_s[..., 1, None, :]
    cand_i = top_i[..., 0, :, None] * PEER_NKEYS + top_i[..., 1, None, :]
    cand_s = cand_s.reshape(B, T, PEER_HEADS, PEER_TOPK * PEER_TOPK)
    cand_i = cand_i.reshape(B, T, PEER_HEADS, PEER_TOPK * PEER_TOPK)
    best_s, pos = lax.top_k(cand_s, PEER_TOPK)
    eidx = jnp.take_along_axis(cand_i, pos, axis=-1)
    gate = jax.nn.softmax(best_s, axis=-1)
    nb = (B * T) // PEER_BLOCK
    E = PEER_HEADS * PEER_TOPK
    xb = h.reshape(nb, PEER_BLOCK, D)
    ib = eidx.reshape(nb, PEER_BLOCK, E)
    gb = gate.reshape(nb, PEER_BLOCK, E).astype(h.dtype)

    def block(args):
        xt, it, gt = args
        ue = u[it]
        ve = v[it]
        a = jax.nn.gelu(jnp.einsum('nd,ned->ne', xt, ue), approximate=False)
        return jnp.einsum('ne,ned->nd', a * gt, ve)

    return lax.map(block, (xb, ib, gb)).reshape(B, T, D)


def setup_inputs(seed: int = 0) -> dict:
    key = jax.random.key(seed)
    ks = jax.random.split(key, 16)
    nrm = lambda k, shape, scale: jax.random.normal(k, shape, jnp.float32) * scale
    return {
        'x': nrm(ks[0], (BATCH, SEQ, D_MODEL), 1.0),
        'mem': nrm(ks[1], (BATCH, MEM_TOKENS, D_MODEL), 1.0),
        'attn_norm_g': 1.0 + nrm(ks[2], (DEPTH, D_MODEL), 0.02),
        'w_in': nrm(ks[3], (DEPTH, D_MODEL, IN_WIDTH), D_MODEL ** -0.5),
        'na_rpb': nrm(ks[4], (DEPTH, NA_HEADS, 2 * NA_WIN_ROWS - 1, 2 * NA_WIN_COLS - 1), 0.1),
        'hg_lower_bound': nrm(ks[5], (2, DEPTH + 1, HG_KWIDTH), 0.1),
        'hg_out_norm_g': 1.0 + nrm(ks[6], (DEPTH, HG_WIDTH), 0.02),
        'mem_norm_g': 1.0 + nrm(ks[7], (DEPTH, D_MODEL), 0.02),
        'w_mem_kv': nrm(ks[8], (DEPTH, D_MODEL, 2 * MEM_WIDTH), D_MODEL ** -0.5),
        'w_out': nrm(ks[9], (DEPTH, MIX_WIDTH, D_MODEL), MIX_WIDTH ** -0.5),
        'ffn_norm_g': 1.0 + nrm(ks[10], (DEPTH, D_MODEL), 0.02),
        'peer_w_query': nrm(ks[11], (DEPTH, D_MODEL, PEER_HEADS * PEER_QDIM), D_MODEL ** -0.5),
        'peer_sub_keys': nrm(ks[12], (DEPTH, PEER_HEADS, 2, PEER_NKEYS, PEER_QDIM // 2), (PEER_QDIM // 2) ** -0.5),
        'peer_u': nrm(ks[13], (DEPTH, N_EXPERTS, D_MODEL), D_MODEL ** -0.5),
        'peer_v': nrm(ks[14], (DEPTH, N_EXPERTS, D_MODEL), 0.5),
        'final_norm_g': 1.0 + nrm(ks[15], (D_MODEL,), 0.02),
    }


def reference(x, mem, attn_norm_g, w_in, na_rpb, hg_lower_bound, hg_out_norm_g, mem_norm_g,
              w_mem_kv, w_out, ffn_norm_g, peer_w_query, peer_sub_keys, peer_u, peer_v,
              final_norm_g):
    B, T, _ = x.shape
    lb_table = jnp.cumsum(jax.nn.softmax(hg_lower_bound.astype(jnp.float32), axis=1), axis=1)
    offsets = np.cumsum(IN_SPLITS)[:-1].tolist()
    for layer in range(DEPTH):
        h = rmsnorm(x, attn_norm_g[layer])
        z = h @ w_in[layer]
        na_q, na_k, na_v, hg_q, hg_ff, hg_fb, hg_i, hg_g, mem_q = jnp.split(z, offsets, axis=-1)
        shp = (B, T, NA_HEADS, NA_HEAD_DIM)
        na_out = neighbourhood_attention(na_q.reshape(shp), na_k.reshape(shp), na_v.reshape(shp),
                                         na_rpb[layer])
        o = hgrn2_bidirectional(hg_q, hg_i, hg_ff, hg_fb, lb_table[0, layer], lb_table[1, layer])
        o = _rms_normalize(o) * hg_out_norm_g[layer].astype(jnp.float32).reshape(HG_HEADS, HG_VAL_DIM)
        hg_out = (o.reshape(B, T, HG_WIDTH) * jax.nn.silu(hg_g.astype(jnp.float32))).astype(x.dtype)
        mem_out = memory_cross_attention(mem_q, rmsnorm(mem, mem_norm_g[layer]), w_mem_kv[layer])
        mix = jnp.concatenate([na_out, hg_out, mem_out], axis=-1)
        x = x + mix @ w_out[layer]
        h2 = rmsnorm(x, ffn_norm_g[layer])
        x = x + peer_ffn(h2, peer_w_query[layer], peer_sub_keys[layer], peer_u[layer], peer_v[layer])
    return rmsnorm(x, final_norm_g)
```

```python
import functools
import math

import numpy as np
import jax
import jax.numpy as jnp
from jax import lax
from jax.experimental import pallas as pl
from jax.experimental.pallas import tpu as pltpu

F32 = jnp.float32
BF16 = jnp.bfloat16

RMS_EPS = 1e-6
GRID_W = 64
NA_HEADS = 8
NA_HEAD_DIM = 64
NA_WIDTH = NA_HEADS * NA_HEAD_DIM
NA_WIN_ROWS = 8
NA_WIN_COLS = 16
NA_GROUP_ROWS = 8
HG_HEADS = 4
HG_DIM = 64
HG_WIDTH = HG_HEADS * HG_DIM
HG_CHUNK = 64
HG_SUB = 8
MEM_HEADS = 4
MEM_WIDTH = 256
PEER_HEADS = 8
PEER_NKEYS = 128
PEER_TOPK = 16
LANE = 128
NEG_BIG = -1e30

VMEM_LIMIT = 48 * 1024 * 1024


def _cparams(sem):
    return pltpu.CompilerParams(dimension_semantics=sem, vmem_limit_bytes=VMEM_LIMIT)


def _rms_scale(x):
    return lax.rsqrt(jnp.mean(x * x, axis=-1, keepdims=True) + RMS_EPS)


def _dot(a, b):
    return jnp.dot(a, b, preferred_element_type=F32)


def _dot_nt(a, b):
    return lax.dot_general(a, b, (((1,), (1,)), ((), ())), preferred_element_type=F32)


def _dot_tn(a, b):
    return lax.dot_general(a, b, (((0,), (0,)), ((), ())), preferred_element_type=F32)


def _in_proj_kernel(x_ref, g_ref, w_ref, qkv_ref, hg_ref, mq_ref):
    x = x_ref[...]
    h = (x * _rms_scale(x) * g_ref[...]).astype(BF16)
    nq = qkv_ref.shape[1]
    nh = hg_ref.shape[1]
    qkv_ref[...] = _dot(h, w_ref[:, :nq]).astype(BF16)
    hg_ref[...] = _dot(h, w_ref[:, nq:nq + nh])
    mq_ref[...] = _dot(h, w_ref[:, nq + nh:]).astype(BF16)


def _in_proj(x2, g, w_bf, *, tm):
    n, d = x2.shape
    nq, nh, nm = 3 * NA_WIDTH, 5 * HG_WIDTH, MEM_WIDTH
    return pl.pallas_call(
        _in_proj_kernel,
        grid=(n // tm,),
        in_specs=[
            pl.BlockSpec((tm, d), lambda i: (i, 0)),
            pl.BlockSpec((1, d), lambda i: (0, 0)),
            pl.BlockSpec((d, nq + nh + nm), lambda i: (0, 0)),
        ],
        out_specs=[
            pl.BlockSpec((tm, nq), lambda i: (i, 0)),
            pl.BlockSpec((tm, nh), lambda i: (i, 0)),
            pl.BlockSpec((tm, nm), lambda i: (i, 0)),
        ],
        out_shape=[
            jax.ShapeDtypeStruct((n, nq), BF16),
            jax.ShapeDtypeStruct((n, nh), F32),
            jax.ShapeDtypeStruct((n, nm), BF16),
        ],
        compiler_params=_cparams(("parallel",)),
        name="in_proj",
    )(x2, g, w_bf)


def _na_bias_table(rpb):
    cols = np.arange(GRID_W)
    c0 = np.clip(cols - NA_WIN_COLS // 2, 0, GRID_W - NA_WIN_COLS)
    kc = np.arange(GRID_W)
    valid = (kc[None, :] >= c0[:, None]) & (kc[None, :] < c0[:, None] + NA_WIN_COLS)
    dc = np.clip(kc[None, :] - cols[:, None] + (NA_WIN_COLS - 1), 0, 2 * NA_WIN_COLS - 2)
    dr = np.arange(NA_WIN_ROWS)[None, :] - np.arange(8)[:, None] + (NA_WIN_ROWS - 1)
    t = rpb.astype(F32)[:, dr[:, :, None, None], dc[None, None, :, :]]
    t = jnp.where(jnp.asarray(valid)[None, None, None], t, NEG_BIG)
    t = t.transpose(1, 0, 3, 2, 4)
    return t.reshape(8, NA_HEADS // 2, 2 * GRID_W, NA_WIN_ROWS * GRID_W)


def _na_kernel(q_ref, kp_ref, kc_ref, kn_ref, vp_ref, vc_ref, vn_ref, bias_ref, o_ref, k_scr, v_scr,
               *, n_rows):
    g = pl.program_id(1)
    gt = NA_GROUP_ROWS * GRID_W
    k_scr[0:gt] = kp_ref[...]
    k_scr[gt:2 * gt] = kc_ref[...]
    k_scr[2 * gt:3 * gt] = kn_ref[...]
    v_scr[0:gt] = vp_ref[...]
    v_scr[gt:2 * gt] = vc_ref[...]
    v_scr[2 * gt:3 * gt] = vn_ref[...]
    lane = lax.broadcasted_iota(jnp.int32, (GRID_W, LANE), 1)
    first_head = lane < NA_HEAD_DIM
    scale = NA_HEAD_DIM ** -0.5

    def row_body(j, carry):
        r = g * NA_GROUP_ROWS + j
        rs = jnp.clip(r - NA_WIN_ROWS // 2, 0, n_rows - NA_WIN_ROWS)
        off = pl.multiple_of((rs - g * NA_GROUP_ROWS + NA_GROUP_ROWS) * GRID_W, GRID_W)
        cls = r - rs
        qoff = pl.multiple_of(j * GRID_W, GRID_W)
        for p in range(NA_HEADS // 2):
            cs = slice(p * LANE, (p + 1) * LANE)
            qp = q_ref[pl.ds(qoff, GRID_W), cs] * scale
            zero = jnp.zeros_like(qp)
            qm = jnp.concatenate([jnp.where(first_head, qp, zero), jnp.where(first_head, zero, qp)], axis=0)
            kw = k_scr[pl.ds(off, NA_WIN_ROWS * GRID_W), cs]
            s = _dot_nt(qm, kw) + bias_ref[cls, p]
            m = jnp.max(s, axis=-1, keepdims=True)
            e = jnp.exp(s - m)
            l = jnp.sum(e, axis=-1, keepdims=True)
            vw = v_scr[pl.ds(off, NA_WIN_ROWS * GRID_W), cs]
            o = _dot(e.astype(BF16), vw) * (1.0 / l)
            o_ref[pl.ds(qoff, GRID_W), cs] = jnp.where(first_head, o[:GRID_W], o[GRID_W:]).astype(o_ref.dtype)
        return carry

    lax.fori_loop(0, NA_GROUP_ROWS, row_body, 0)


def _na(qkv, bias, *, batch, n_rows):
    n = qkv.shape[0]
    gt = NA_GROUP_ROWS * GRID_W
    ng = n_rows // NA_GROUP_ROWS
    w = NA_WIDTH

    def spec(col, shift):
        def imap(b, g):
            return (b * ng + jnp.clip(g + shift, 0, ng - 1), col)
        return pl.BlockSpec((gt, w), imap)

    return pl.pallas_call(
        functools.partial(_na_kernel, n_rows=n_rows),
        grid=(batch, ng),
        in_specs=[spec(0, 0), spec(1, -1), spec(1, 0), spec(1, 1), spec(2, -1), spec(2, 0), spec(2, 1),
                  pl.BlockSpec(bias.shape, lambda b, g: (0, 0, 0, 0))],
        out_specs=pl.BlockSpec((gt, w), lambda b, g: (b * ng + g, 0)),
        out_shape=jax.ShapeDtypeStruct((n, w), BF16),
        scratch_shapes=[pltpu.VMEM((3 * gt, w), BF16), pltpu.VMEM((3 * gt, w), BF16)],
        compiler_params=_cparams(("parallel", "parallel")),
        name="na",
    )(qkv, qkv, qkv, qkv, qkv, qkv, qkv, bias)


def _hg_constants():
    c, sub = HG_CHUNK, HG_SUB
    t = np.arange(c)[:, None]
    s = np.arange(c)[None, :]
    i8 = np.arange(sub)[:, None]
    mats = []
    for rev in (False, True):
        if not rev:
            inc = s <= t
            rb = s < sub * (t // sub)
            eb = s <= sub * (t // sub) + sub - 1
            r8 = s < sub * i8
        else:
            inc = s >= t
            rb = s > sub * (t // sub) + sub - 1
            eb = s >= sub * (t // sub)
            r8 = s > sub * i8 + sub - 1
        extra = np.zeros((16, c), bool)
        extra[:8] = r8
        extra[8] = True
        mats.append(np.concatenate([inc, rb, eb, extra], axis=0))
    tmat = np.stack(mats).astype(np.float32)
    lane_head = np.arange(HG_WIDTH) // HG_DIM
    blockdiag = (lane_head[:, None] == lane_head[None, :]).astype(np.float32)
    return tmat, blockdiag


def _hg_chunk(q, fpre, v, lb, st, tmat, blockones, rev):
    c, sub, nsub, w = HG_CHUNK, HG_SUB, HG_CHUNK // HG_SUB, HG_WIDTH
    f = lb + (1.0 - lb) * jax.nn.sigmoid(fpre)
    kk = 1.0 - f
    lf = jnp.log(f)
    lf_hi = lf.astype(BF16)
    lf_lo = (lf - lf_hi.astype(F32)).astype(BF16)
    cs = _dot(tmat, lf_hi) + _dot(tmat, lf_lo)
    b, rb, eb = cs[0:c], cs[c:2 * c], cs[2 * c:3 * c]
    r8 = cs[3 * c:3 * c + nsub]
    tot = cs[3 * c + nsub:3 * c + nsub + 1]

    lane = lax.broadcasted_iota(jnp.int32, (1, w), 1)
    head_masks = [(lane // HG_DIM) == h for h in range(HG_HEADS)]
    col_sub = (lane % HG_DIM) // sub

    qt = q * jnp.exp(b - rb)
    kt = kk * jnp.exp(eb - b)

    blocks, pairs = [], []
    for i in range(nsub):
        js = range(i) if not rev else range(i + 1, nsub)
        if len(js) == 0:
            continue
        mid = jnp.exp(r8[i:i + 1] - r8)
        for j in js:
            m = j + 1 if not rev else j - 1
            blocks.append(qt[sub * i:sub * (i + 1)] * mid[m:m + 1])
            pairs.append((i, j))
    qbig = jnp.concatenate(blocks, axis=0).astype(BF16)
    zero = jnp.zeros_like(kt)
    kstack = jnp.concatenate([jnp.where(hm, kt, zero) for hm in head_masks], axis=0).astype(BF16)
    roff = _dot_nt(qbig, kstack)
    rows = [jnp.zeros((sub, w), F32)] * nsub
    for n, (i, j) in enumerate(pairs):
        rows[i] = rows[i] + jnp.where(col_sub == j, roff[sub * n:sub * (n + 1)], 0.0)
    a_off = jnp.concatenate(rows, axis=0).astype(BF16)
    vstack = jnp.concatenate([jnp.where(hm, v, zero) for hm in head_masks], axis=0).astype(BF16)
    o = _dot(a_off, vstack)

    row_in_sub = lax.broadcasted_iota(jnp.int32, (nsub, sub, w), 1)
    b3, kk3, v3 = b.reshape(nsub, sub, w), kk.reshape(nsub, sub, w), v.reshape(nsub, sub, w)
    q3 = q.reshape(nsub, sub, w)
    dparts, vparts = [q3 * kk3], [v3]
    for d in range(1, sub):
        shift = d if not rev else sub - d
        valid = (row_in_sub >= d) if not rev else (row_in_sub + d < sub)
        bd = pltpu.roll(b3, shift, axis=1)
        kd = pltpu.roll(kk3, shift, axis=1)
        vparts.append(pltpu.roll(v3, shift, axis=1))
        ex = jnp.where(valid, b3 - bd, 0.0)
        dparts.append(jnp.where(valid, q3 * kd * jnp.exp(ex), 0.0))
    dstack = jnp.concatenate(dparts, axis=0).reshape(sub * c, w).astype(BF16)
    abc = _dot(dstack, blockones).reshape(sub, nsub, sub, w)
    for d in range(sub):
        o = o + (abc[d] * vparts[d]).reshape(c, w)

    o = o + _dot_nt((q * jnp.exp(b)).astype(BF16), st.astype(BF16))
    kdec = (kk * jnp.exp(tot - b)).astype(BF16)
    ut = _dot_tn(v.astype(BF16), kdec)
    st_new = st * jnp.exp(tot) + ut * blockones.astype(F32)
    return o, st_new


def _hg_kernel(qf_ref, ff_ref, vf_ref, qb_ref, fb_ref, vb_ref, lb_ref, tmat_ref, ones_ref,
               of_ref, ob_ref, sf_ref, sb_ref, *, chunks):
    @pl.when(pl.program_id(1) == 0)
    def _():
        sf_ref[...] = jnp.zeros_like(sf_ref)
        sb_ref[...] = jnp.zeros_like(sb_ref)

    blockones = ones_ref[...]
    lbf, lbb = lb_ref[0:1], lb_ref[1:2]

    def body(ci, carry):
        rf = pl.ds(pl.multiple_of(ci * HG_CHUNK, HG_CHUNK), HG_CHUNK)
        rb = pl.ds(pl.multiple_of((chunks - 1 - ci) * HG_CHUNK, HG_CHUNK), HG_CHUNK)
        o_f, s_f = _hg_chunk(qf_ref[rf], ff_ref[rf], vf_ref[rf], lbf, sf_ref[...], tmat_ref[0], blockones, False)
        of_ref[rf] = o_f
        sf_ref[...] = s_f
        o_b, s_b = _hg_chunk(qb_ref[rb], fb_ref[rb], vb_ref[rb], lbb, sb_ref[...], tmat_ref[1], blockones, True)
        ob_ref[rb] = o_b
        sb_ref[...] = s_b
        return carry

    lax.fori_loop(0, chunks, body, 0)


def _hgrn2(hg, lb2, *, batch, seq, chunks):
    n = hg.shape[0]
    tb = chunks * HG_CHUNK
    nb = seq // tb
    w = HG_WIDTH
    tmat, blockdiag = _hg_constants()
    tmat = jnp.asarray(tmat, BF16)
    blockones = jnp.asarray(blockdiag, BF16)

    def fwd(col):
        return pl.BlockSpec((tb, w), lambda b, i: (b * nb + i, col))

    def bwd(col):
        return pl.BlockSpec((tb, w), lambda b, i: (b * nb + nb - 1 - i, col))

    return pl.pallas_call(
        functools.partial(_hg_kernel, chunks=chunks),
        grid=(batch, nb),
        in_specs=[fwd(0), fwd(1), fwd(3), bwd(0), bwd(2), bwd(3),
                  pl.BlockSpec((2, w), lambda b, i: (0, 0)),
                  pl.BlockSpec(tmat.shape, lambda b, i: (0, 0, 0)),
                  pl.BlockSpec((w, w), lambda b, i: (0, 0))],
        out_specs=[fwd(0), bwd(0)],
        out_shape=[jax.ShapeDtypeStruct((n, w), F32), jax.ShapeDtypeStruct((n, w), F32)],
        scratch_shapes=[pltpu.VMEM((w, w), F32), pltpu.VMEM((w, w), F32)],
        compiler_params=_cparams(("parallel", "arbitrary")),
        name="hgrn2",
    )(hg, hg, hg, hg, hg, hg, lb2, tmat, blockones)


def _mem_kv_kernel(m_ref, g_ref, w_ref, o_ref):
    m = m_ref[...]
    h = (m * _rms_scale(m) * g_ref[...]).astype(BF16)
    o_ref[...] = _dot(h, w_ref[...]).astype(BF16)


def _mem_kv(mem2, g, w_bf, *, batch):
    n, d = mem2.shape
    m = n // batch
    return pl.pallas_call(
        _mem_kv_kernel,
        grid=(batch,),
        in_specs=[pl.BlockSpec((m, d), lambda b: (b, 0)),
                  pl.BlockSpec((1, d), lambda b: (0, 0)),
                  pl.BlockSpec(w_bf.shape, lambda b: (0, 0))],
        out_specs=pl.BlockSpec((m, w_bf.shape[1]), lambda b: (b, 0)),
        out_shape=jax.ShapeDtypeStruct((n, w_bf.shape[1]), BF16),
        compiler_params=_cparams(("parallel",)),
        name="mem_kv",
    )(mem2, g, w_bf)


def _mix_kernel(na_ref, of_ref, ob_ref, gate_ref, mq_ref, kv_ref, x_ref, hgn_ref, ones_ref, wout_ref,
                fg_ref, wq_ref, sk_ref, x1_ref, h2_ref, st_ref):
    w = HG_WIDTH
    lane = lax.broadcasted_iota(jnp.int32, (1, w), 1)
    head_masks = [(lane // HG_DIM) == h for h in range(HG_HEADS)]
    ones = ones_ref[...]

    o = of_ref[...] + ob_ref[...]
    o2 = o * o
    o2_hi = o2.astype(BF16)
    o2_lo = (o2 - o2_hi.astype(F32)).astype(BF16)
    ms = (_dot(o2_hi, ones) + _dot(o2_lo, ones)) * (1.0 / HG_DIM)
    g = gate_ref[...]
    hg_out = o * lax.rsqrt(ms + RMS_EPS) * hgn_ref[...] * (g * jax.nn.sigmoid(g))

    mq = mq_ref[...] * (MEM_WIDTH // MEM_HEADS) ** -0.5
    mk = kv_ref[:, :MEM_WIDTH]
    mv = kv_ref[:, MEM_WIDTH:]
    zero = jnp.zeros_like(mq)
    mem_out = jnp.zeros(mq.shape, F32)
    for hm in head_masks:
        s = _dot_nt(jnp.where(hm, mq, zero), mk)
        e = jnp.exp(s - jnp.max(s, axis=-1, keepdims=True))
        r = _dot(e.astype(BF16), mv) * (1.0 / jnp.sum(e, axis=-1, keepdims=True))
        mem_out = mem_out + jnp.where(hm, r, 0.0)

    nw = NA_WIDTH
    x1 = (x_ref[...] + _dot(na_ref[...], wout_ref[:nw])
          + _dot(hg_out.astype(BF16), wout_ref[nw:nw + w])
          + _dot(mem_out.astype(BF16), wout_ref[nw + w:]))
    x1_ref[...] = x1
    h2 = (x1 * _rms_scale(x1) * fg_ref[...]).astype(BF16)
    h2_ref[...] = h2
    qp = _dot(h2, wq_ref[...]).astype(BF16)
    for hc in range(2 * PEER_HEADS):
        st_ref[hc] = _dot_nt(sk_ref[hc], qp[:, hc * PEER_NKEYS:(hc + 1) * PEER_NKEYS])


def _mix(na_out, o_f, o_b, hg, mq, kv, x2, hgn_g, wout_bf, ffn_g, wq_bf, sk_bf, *, seq, tm):
    n, d = x2.shape
    w = HG_WIDTH
    per_batch = seq // tm
    mem_tokens = kv.shape[0] // (n // seq)
    _, blockdiag = _hg_constants()
    blockones = jnp.asarray(blockdiag, BF16)
    nhc = 2 * PEER_HEADS

    def row(width, col=0):
        return pl.BlockSpec((tm, width), lambda i: (i, col))

    def full(a):
        return pl.BlockSpec(a.shape, lambda i: (0,) * a.ndim)

    return pl.pallas_call(
        _mix_kernel,
        grid=(n // tm,),
        in_specs=[row(NA_WIDTH), row(w), row(w), row(w, 4), row(MEM_WIDTH),
                  pl.BlockSpec((mem_tokens, kv.shape[1]), lambda i: (i // per_batch, 0)),
                  row(d), full(hgn_g), full(blockones), full(wout_bf), full(ffn_g), full(wq_bf), full(sk_bf)],
        out_specs=[row(d), row(d), pl.BlockSpec((nhc, PEER_NKEYS, tm), lambda i: (0, 0, i))],
        out_shape=[jax.ShapeDtypeStruct((n, d), F32), jax.ShapeDtypeStruct((n, d), BF16),
                   jax.ShapeDtypeStruct((nhc, PEER_NKEYS, n), F32)],
        compiler_params=_cparams(("parallel",)),
        name="mix",
    )(na_out, o_f, o_b, hg, mq, kv, x2, hgn_g, blockones, wout_bf, ffn_g, wq_bf, sk_bf)


def _top_values(s, count, want_rank):
    cur = s
    vals = []
    rank = jnp.full(s.shape, float(count), F32) if want_rank else None
    for k in range(count):
        m = jnp.max(cur, axis=0, keepdims=True)
        vals.append(m)
        hit = cur == m
        if want_rank:
            rank = jnp.where(hit, float(k), rank)
        if k + 1 < count:
            cur = jnp.where(hit, -jnp.inf, cur)
    return vals, rank


def _route_kernel(st_ref, r2_ref, e2_ref, l1_ref, e1_ref):
    k = PEER_TOPK
    for h in range(PEER_HEADS):
        s1 = st_ref[2 * h]
        s2 = st_ref[2 * h + 1]
        v1, _ = _top_values(s1, k, False)
        v2, r2 = _top_values(s2, k, True)
        cand = [v1[a] + v2[b] for a in range(k) for b in range(k // (a + 1))]
        pad = (-len(cand)) % 8
        cand = jnp.concatenate(cand + [jnp.full_like(cand[0], -jnp.inf)] * pad, axis=0)
        best, _ = _top_values(cand, k, False)
        tau = best[-1]
        z = best[0] * 0.0
        for c in best:
            z = z + jnp.exp(c - best[0])
        cnt = jnp.zeros(s1.shape, F32)
        for b in range(k):
            cnt = cnt + jnp.where(s1 + v2[b] >= tau, 1.0, 0.0)
        r2_ref[h] = r2.astype(BF16)
        e2_ref[h] = jnp.exp(s2 - v2[0]).astype(BF16)
        l1_ref[h] = cnt
        e1_ref[h] = jnp.exp(s1 - v1[0]) * (1.0 / z)


def _route(st, *, tt):
    nhc, nk, n = st.shape
    heads = nhc // 2
    spec = pl.BlockSpec((heads, nk, tt), lambda i: (0, 0, i))
    return pl.pallas_call(
        _route_kernel,
        grid=(n // tt,),
        in_specs=[pl.BlockSpec((nhc, nk, tt), lambda i: (0, 0, i))],
        out_specs=[spec, spec, spec, spec],
        out_shape=[jax.ShapeDtypeStruct((heads, nk, n), BF16), jax.ShapeDtypeStruct((heads, nk, n), BF16),
                   jax.ShapeDtypeStruct((heads, nk, n), F32), jax.ShapeDtypeStruct((heads, nk, n), F32)],
        compiler_params=_cparams(("parallel",)),
        name="route",
    )(st)


def _peer_kernel(h2_ref, u_ref, vt_ref, r2_ref, e2_ref, l1_ref, e1_ref, x1_ref, fg_ref, o_ref,
                 a_scr, w_scr, acc_scr):
    j = pl.program_id(1)
    nk = PEER_NKEYS
    groups = u_ref.shape[0] // nk

    @pl.when(j == 0)
    def _():
        acc_scr[...] = jnp.zeros_like(acc_scr)

    a_scr[...] = _dot_nt(u_ref[...], h2_ref[...])
    for a in range(groups):
        gate = None
        for h in range(PEER_HEADS):
            limit = l1_ref[h, a:a + 1, :].astype(BF16)
            weight = e1_ref[h, a:a + 1, :].astype(BF16)
            term = jnp.where(r2_ref[h] < limit, e2_ref[h], jnp.zeros((), BF16)) * weight
            gate = term if gate is None else gate + term
        rows = slice(a * nk, (a + 1) * nk)
        act = a_scr[rows]
        act = 0.5 * act * (1.0 + lax.erf(act * (2.0 ** -0.5)))
        w_scr[rows] = act.astype(BF16) * gate
    acc_scr[...] += _dot(vt_ref[...], w_scr[...])

    @pl.when(j == pl.num_programs(1) - 1)
    def _():
        y = x1_ref[...] + acc_scr[...].T
        o_ref[...] = y * _rms_scale(y) * fg_ref[...]


def _peer(h2, u_bf, vt_bf, r2, e2, l1, e1, x1, final_g, *, tn, ec):
    n, d = x1.shape
    n_exp = u_bf.shape[0]
    groups = ec // PEER_NKEYS
    tab = pl.BlockSpec((PEER_HEADS, PEER_NKEYS, tn), lambda i, j: (0, 0, i))
    sel = pl.BlockSpec((PEER_HEADS, groups, tn), lambda i, j: (0, j, i))
    return pl.pallas_call(
        _peer_kernel,
        grid=(n // tn, n_exp // ec),
        in_specs=[pl.BlockSpec((tn, d), lambda i, j: (i, 0)),
                  pl.BlockSpec((ec, d), lambda i, j: (j, 0)),
                  pl.BlockSpec((d, ec), lambda i, j: (0, j)),
                  tab, tab, sel, sel,
                  pl.BlockSpec((tn, d), lambda i, j: (i, 0)),
                  pl.BlockSpec((1, d), lambda i, j: (0, 0))],
        out_specs=pl.BlockSpec((tn, d), lambda i, j: (i, 0)),
        out_shape=jax.ShapeDtypeStruct((n, d), F32),
        scratch_shapes=[pltpu.VMEM((ec, tn), F32), pltpu.VMEM((ec, tn), BF16), pltpu.VMEM((d, tn), F32)],
        compiler_params=_cparams(("parallel", "arbitrary")),
        name="peer",
    )(h2, u_bf, vt_bf, r2, e2, l1, e1, x1, final_g)


def kernel(x, mem, attn_norm_g, w_in, na_rpb, hg_lower_bound, hg_out_norm_g, mem_norm_g, w_mem_kv, w_out,
           ffn_norm_g, peer_w_query, peer_sub_keys, peer_u, peer_v, final_norm_g):
    batch, seq, d = x.shape
    n = batch * seq
    x2 = x.reshape(n, d)
    lb_table = jnp.cumsum(jax.nn.softmax(hg_lower_bound.astype(F32), axis=1), axis=1)
    qkv, hg, mq = _in_proj(x2, attn_norm_g[0].reshape(1, d), w_in[0].astype(BF16), tm=512)
    na_out = _na(qkv, _na_bias_table(na_rpb[0]), batch=batch, n_rows=seq // GRID_W)
    o_f, o_b = _hgrn2(hg, lb_table[:, 0], batch=batch, seq=seq, chunks=4)
    kv = _mem_kv(mem.reshape(-1, d), mem_norm_g[0].reshape(1, d), w_mem_kv[0].astype(BF16), batch=batch)
    sk = peer_sub_keys[0].reshape(2 * PEER_HEADS, PEER_NKEYS, -1).astype(BF16)
    x1, h2, st = _mix(na_out, o_f, o_b, hg, mq, kv, x2, hg_out_norm_g[0].reshape(1, -1), w_out[0].astype(BF16),
                      ffn_norm_g[0].reshape(1, d), peer_w_query[0].astype(BF16), sk, seq=seq, tm=512)
    r2, e2, l1, e1 = _route(st, tt=256)
    out = _peer(h2, peer_u[0].astype(BF16), peer_v[0].T.astype(BF16), r2, e2, l1, e1, x1,
                final_norm_g.reshape(1, d), tn=512, ec=1024)
    return out.reshape(batch, seq, d)
```

```python
import functools
import math

import numpy as np
import jax
import jax.numpy as jnp
from jax import lax
from jax.experimental import pallas as pl
from jax.experimental.pallas import tpu as pltpu

F32 = jnp.float32
BF16 = jnp.bfloat16

RMS_EPS = 1e-6
GRID_W = 64
NA_HEADS = 8
NA_HEAD_DIM = 64
NA_WIDTH = NA_HEADS * NA_HEAD_DIM
NA_WIN_ROWS = 8
NA_WIN_COLS = 16
NA_GROUP_ROWS = 8
HG_HEADS = 4
HG_DIM = 64
HG_WIDTH = HG_HEADS * HG_DIM
HG_CHUNK = 64
HG_SUB = 8
MEM_HEADS = 4
MEM_WIDTH = 256
PEER_HEADS = 8
PEER_NKEYS = 128
PEER_TOPK = 16
LANE = 128
NEG_BIG = -1e30

VMEM_LIMIT = 48 * 1024 * 1024


def _cparams(sem):
    return pltpu.CompilerParams(dimension_semantics=sem, vmem_limit_bytes=VMEM_LIMIT)


def _rms_scale(x):
    return lax.rsqrt(jnp.mean(x * x, axis=-1, keepdims=True) + RMS_EPS)


def _dot(a, b):
    return jnp.dot(a, b, preferred_element_type=F32)


def _dot_nt(a, b):
    return lax.dot_general(a, b, (((1,), (1,)), ((), ())), preferred_element_type=F32)


def _dot_tn(a, b):
    return lax.dot_general(a, b, (((0,), (0,)), ((), ())), preferred_element_type=F32)


def _in_proj_kernel(x_ref, g_ref, w_ref, qkv_ref, hg_ref, mq_ref):
    x = x_ref[...]
    h = (x * _rms_scale(x) * g_ref[...]).astype(BF16)
    nq = qkv_ref.shape[1]
    nh = hg_ref.shape[1]
    qkv_ref[...] = _dot(h, w_ref[:, :nq]).astype(BF16)
    hg_ref[...] = _dot(h, w_ref[:, nq:nq + nh])
    mq_ref[...] = _dot(h, w_ref[:, nq + nh:]).astype(BF16)


def _in_proj(x2, g, w_bf, *, tm):
    n, d = x2.shape
    nq, nh, nm = 3 * NA_WIDTH, 5 * HG_WIDTH, MEM_WIDTH
    return pl.pallas_call(
        _in_proj_kernel,
        grid=(n // tm,),
        in_specs=[
            pl.BlockSpec((tm, d), lambda i: (i, 0)),
            pl.BlockSpec((1, d), lambda i: (0, 0)),
            pl.BlockSpec((d, nq + nh + nm), lambda i: (0, 0)),
        ],
        out_specs=[
            pl.BlockSpec((tm, nq), lambda i: (i, 0)),
            pl.BlockSpec((tm, nh), lambda i: (i, 0)),
            pl.BlockSpec((tm, nm), lambda i: (i, 0)),
        ],
        out_shape=[
            jax.ShapeDtypeStruct((n, nq), BF16),
            jax.ShapeDtypeStruct((n, nh), F32),
            jax.ShapeDtypeStruct((n, nm), BF16),
        ],
        compiler_params=_cparams(("parallel",)),
        name="in_proj",
    )(x2, g, w_bf)


def _na_bias_table(rpb):
    cols = np.arange(GRID_W)
    c0 = np.clip(cols - NA_WIN_COLS // 2, 0, GRID_W - NA_WIN_COLS)
    kc = np.arange(GRID_W)
    valid = (kc[None, :] >= c0[:, None]) & (kc[None, :] < c0[:, None] + NA_WIN_COLS)
    dc = kc[None, :] - cols[:, None] + (NA_WIN_COLS - 1)
    onehot = (dc[None] == np.arange(2 * NA_WIN_COLS - 1)[:, None, None]) & valid[None]
    band = jnp.einsum('hrd,dqk->hrqk', rpb.astype(F32), jnp.asarray(onehot, F32),
                      precision=lax.Precision.HIGHEST)
    band = band + jnp.asarray(np.where(valid, 0.0, NEG_BIG), F32)
    t = jnp.concatenate([band[:, :-1], band[:, 1:]], axis=-1)
    n_e = t.shape[1]
    t = t.reshape(NA_HEADS // 2, 2, n_e, GRID_W, 2 * GRID_W).transpose(0, 2, 1, 3, 4)
    return t.reshape(NA_HEADS // 2, n_e, 2 * GRID_W, 2 * GRID_W)


def _na_kernel(q_ref, kp_ref, kc_ref, kn_ref, vp_ref, vc_ref, vn_ref, bias_ref, o_ref, k_scr, v_scr,
               *, n_rows):
    g = pl.program_id(1)
    gt = NA_GROUP_ROWS * GRID_W
    k_scr[0:gt] = kp_ref[...]
    k_scr[gt:2 * gt] = kc_ref[...]
    k_scr[2 * gt:3 * gt] = kn_ref[...]
    v_scr[0:gt] = vp_ref[...]
    v_scr[gt:2 * gt] = vc_ref[...]
    v_scr[2 * gt:3 * gt] = vn_ref[...]
    lane = lax.broadcasted_iota(jnp.int32, (GRID_W, LANE), 1)
    first_head = lane < NA_HEAD_DIM
    scale = NA_HEAD_DIM ** -0.5

    def row_body(j, carry):
        r = g * NA_GROUP_ROWS + j
        rs = jnp.clip(r - NA_WIN_ROWS // 2, 0, n_rows - NA_WIN_ROWS)
        off = pl.multiple_of((rs - g * NA_GROUP_ROWS + NA_GROUP_ROWS) * GRID_W, GRID_W)
        cls = r - rs
        qoff = pl.multiple_of(j * GRID_W, GRID_W)
        for p in range(NA_HEADS // 2):
            cs = slice(p * LANE, (p + 1) * LANE)
            qp = q_ref[pl.ds(qoff, GRID_W), cs] * scale
            zero = jnp.zeros_like(qp)
            qm = jnp.concatenate([jnp.where(first_head, qp, zero), jnp.where(first_head, zero, qp)], axis=0)
            kw = k_scr[pl.ds(off, NA_WIN_ROWS * GRID_W), cs]
            bias = jnp.concatenate(
                [bias_ref[p, NA_WIN_ROWS - 1 - cls + 2 * m] for m in range(NA_WIN_ROWS // 2)], axis=1)
            s = _dot_nt(qm, kw) + bias
            m = jnp.max(s, axis=-1, keepdims=True)
            e = jnp.exp(s - m)
            l = jnp.sum(e, axis=-1, keepdims=True)
            vw = v_scr[pl.ds(off, NA_WIN_ROWS * GRID_W), cs]
            o = _dot(e.astype(BF16), vw) * (1.0 / l)
            o_ref[pl.ds(qoff, GRID_W), cs] = jnp.where(first_head, o[:GRID_W], o[GRID_W:]).astype(o_ref.dtype)
        return carry

    lax.fori_loop(0, NA_GROUP_ROWS, row_body, 0)


def _na(qkv, bias, *, batch, n_rows):
    n = qkv.shape[0]
    gt = NA_GROUP_ROWS * GRID_W
    ng = n_rows // NA_GROUP_ROWS
    w = NA_WIDTH

    def spec(col, shift):
        def imap(b, g):
            return (b * ng + jnp.clip(g + shift, 0, ng - 1), col)
        return pl.BlockSpec((gt, w), imap)

    return pl.pallas_call(
        functools.partial(_na_kernel, n_rows=n_rows),
        grid=(batch, ng),
        in_specs=[spec(0, 0), spec(1, -1), spec(1, 0), spec(1, 1), spec(2, -1), spec(2, 0), spec(2, 1),
                  pl.BlockSpec(bias.shape, lambda b, g: (0, 0, 0, 0))],
        out_specs=pl.BlockSpec((gt, w), lambda b, g: (b * ng + g, 0)),
        out_shape=jax.ShapeDtypeStruct((n, w), BF16),
        scratch_shapes=[pltpu.VMEM((3 * gt, w), BF16), pltpu.VMEM((3 * gt, w), BF16)],
        compiler_params=_cparams(("parallel", "parallel")),
        name="na",
    )(qkv, qkv, qkv, qkv, qkv, qkv, qkv, bias)


def _hg_constants():
    c, sub = HG_CHUNK, HG_SUB
    t = np.arange(c)[:, None]
    s = np.arange(c)[None, :]
    i8 = np.arange(sub)[:, None]
    mats = []
    for rev in (False, True):
        if not rev:
            inc = s <= t
            rb = s < sub * (t // sub)
            eb = s <= sub * (t // sub) + sub - 1
            r8 = s < sub * i8
        else:
            inc = s >= t
            rb = s > sub * (t // sub) + sub - 1
            eb = s >= sub * (t // sub)
            r8 = s > sub * i8 + sub - 1
        extra = np.zeros((16, c), bool)
        extra[:8] = r8
        extra[8] = True
        mats.append(np.concatenate([inc, rb, eb, extra], axis=0))
    tmat = np.stack(mats).astype(np.float32)
    lane_head = np.arange(HG_WIDTH) // HG_DIM
    blockdiag = (lane_head[:, None] == lane_head[None, :]).astype(np.float32)
    return tmat, blockdiag


def _hg_chunk(q, fpre, v, lb, st, tmat, blockones, rev):
    c, sub, nsub, w = HG_CHUNK, HG_SUB, HG_CHUNK // HG_SUB, HG_WIDTH
    f = lb + (1.0 - lb) * jax.nn.sigmoid(fpre)
    kk = 1.0 - f
    lf = jnp.log(f)
    lf_hi = lf.astype(BF16)
    lf_lo = (lf - lf_hi.astype(F32)).astype(BF16)
    cs = _dot(tmat, lf_hi) + _dot(tmat, lf_lo)
    b, rb, eb = cs[0:c], cs[c:2 * c], cs[2 * c:3 * c]
    r8 = cs[3 * c:3 * c + nsub]
    tot = cs[3 * c + nsub:3 * c + nsub + 1]

    lane = lax.broadcasted_iota(jnp.int32, (1, w), 1)
    head_masks = [(lane // HG_DIM) == h for h in range(HG_HEADS)]
    col_sub = (lane % HG_DIM) // sub

    qt = q * jnp.exp(b - rb)
    kt = kk * jnp.exp(eb - b)

    blocks, pairs = [], []
    for i in range(nsub):
        js = range(i) if not rev else range(i + 1, nsub)
        if len(js) == 0:
            continue
        mid = jnp.exp(r8[i:i + 1] - r8)
        for j in js:
            m = j + 1 if not rev else j - 1
            blocks.append(qt[sub * i:sub * (i + 1)] * mid[m:m + 1])
            pairs.append((i, j))
    qbig = jnp.concatenate(blocks, axis=0).astype(BF16)
    zero = jnp.zeros_like(kt)
    kstack = jnp.concatenate([jnp.where(hm, kt, zero) for hm in head_masks], axis=0).astype(BF16)
    roff = _dot_nt(qbig, kstack)
    rows = [jnp.zeros((sub, w), F32)] * nsub
    for n, (i, j) in enumerate(pairs):
        rows[i] = rows[i] + jnp.where(col_sub == j, roff[sub * n:sub * (n + 1)], 0.0)
    a_off = jnp.concatenate(rows, axis=0).astype(BF16)
    vstack = jnp.concatenate([jnp.where(hm, v, zero) for hm in head_masks], axis=0).astype(BF16)
    o = _dot(a_off, vstack)

    row_in_sub = lax.broadcasted_iota(jnp.int32, (nsub, sub, w), 1)
    b3, kk3, v3 = b.reshape(nsub, sub, w), kk.reshape(nsub, sub, w), v.reshape(nsub, sub, w)
    q3 = q.reshape(nsub, sub, w)
    dparts, vparts = [q3 * kk3], [v3]
    for d in range(1, sub):
        shift = d if not rev else sub - d
        valid = (row_in_sub >= d) if not rev else (row_in_sub + d < sub)
        bd = pltpu.roll(b3, shift, axis=1)
        kd = pltpu.roll(kk3, shift, axis=1)
        vparts.append(pltpu.roll(v3, shift, axis=1))
        ex = jnp.where(valid, b3 - bd, 0.0)
        dparts.append(jnp.where(valid, q3 * kd * jnp.exp(ex), 0.0))
    dstack = jnp.concatenate(dparts, axis=0).reshape(sub * c, w).astype(BF16)
    abc = _dot(dstack, blockones).reshape(sub, nsub, sub, w)
    for d in range(sub):
        o = o + (abc[d] * vparts[d]).reshape(c, w)

    o = o + _dot_nt((q * jnp.exp(b)).astype(BF16), st.astype(BF16))
    kdec = (kk * jnp.exp(tot - b)).astype(BF16)
    ut = _dot_tn(v.astype(BF16), kdec)
    st_new = st * jnp.exp(tot) + ut * blockones.astype(F32)
    return o, st_new


def _hg_kernel(qf_ref, ff_ref, vf_ref, qb_ref, fb_ref, vb_ref, lb_ref, tmat_ref, ones_ref,
               of_ref, ob_ref, sf_ref, sb_ref, *, chunks):
    @pl.when(pl.program_id(1) == 0)
    def _():
        sf_ref[...] = jnp.zeros_like(sf_ref)
        sb_ref[...] = jnp.zeros_like(sb_ref)

    blockones = ones_ref[...]
    lbf, lbb = lb_ref[0:1], lb_ref[1:2]

    def body(ci, carry):
        rf = pl.ds(pl.multiple_of(ci * HG_CHUNK, HG_CHUNK), HG_CHUNK)
        rb = pl.ds(pl.multiple_of((chunks - 1 - ci) * HG_CHUNK, HG_CHUNK), HG_CHUNK)
        o_f, s_f = _hg_chunk(qf_ref[rf], ff_ref[rf], vf_ref[rf], lbf, sf_ref[...], tmat_ref[0], blockones, False)
        of_ref[rf] = o_f
        sf_ref[...] = s_f
        o_b, s_b = _hg_chunk(qb_ref[rb], fb_ref[rb], vb_ref[rb], lbb, sb_ref[...], tmat_ref[1], blockones, True)
        ob_ref[rb] = o_b
        sb_ref[...] = s_b
        return carry

    lax.fori_loop(0, chunks, body, 0)


def _hgrn2(hg, lb2, *, batch, seq, chunks):
    n = hg.shape[0]
    tb = chunks * HG_CHUNK
    nb = seq // tb
    w = HG_WIDTH
    tmat, blockdiag = _hg_constants()
    tmat = jnp.asarray(tmat, BF16)
    blockones = jnp.asarray(blockdiag, BF16)

    def fwd(col):
        return pl.BlockSpec((tb, w), lambda b, i: (b * nb + i, col))

    def bwd(col):
        return pl.BlockSpec((tb, w), lambda b, i: (b * nb + nb - 1 - i, col))

    return pl.pallas_call(
        functools.partial(_hg_kernel, chunks=chunks),
        grid=(batch, nb),
        in_specs=[fwd(0), fwd(1), fwd(3), bwd(0), bwd(2), bwd(3),
                  pl.BlockSpec((2, w), lambda b, i: (0, 0)),
                  pl.BlockSpec(tmat.shape, lambda b, i: (0, 0, 0)),
                  pl.BlockSpec((w, w), lambda b, i: (0, 0))],
        out_specs=[fwd(0), bwd(0)],
        out_shape=[jax.ShapeDtypeStruct((n, w), F32), jax.ShapeDtypeStruct((n, w), F32)],
        scratch_shapes=[pltpu.VMEM((w, w), F32), pltpu.VMEM((w, w), F32)],
        compiler_params=_cparams(("parallel", "arbitrary")),
        name="hgrn2",
    )(hg, hg, hg, hg, hg, hg, lb2, tmat, blockones)


def _mem_kv_kernel(m_ref, g_ref, w_ref, o_ref):
    m = m_ref[...]
    h = (m * _rms_scale(m) * g_ref[...]).astype(BF16)
    o_ref[...] = _dot(h, w_ref[...]).astype(BF16)


def _mem_kv(mem2, g, w_bf, *, batch):
    n, d = mem2.shape
    m = n // batch
    return pl.pallas_call(
        _mem_kv_kernel,
        grid=(batch,),
        in_specs=[pl.BlockSpec((m, d), lambda b: (b, 0)),
                  pl.BlockSpec((1, d), lambda b: (0, 0)),
                  pl.BlockSpec(w_bf.shape, lambda b: (0, 0))],
        out_specs=pl.BlockSpec((m, w_bf.shape[1]), lambda b: (b, 0)),
        out_shape=jax.ShapeDtypeStruct((n, w_bf.shape[1]), BF16),
        compiler_params=_cparams(("parallel",)),
        name="mem_kv",
    )(mem2, g, w_bf)


def _mix_kernel(na_ref, of_ref, ob_ref, gate_ref, mq_ref, kv_ref, x_ref, hgn_ref, ones_ref, wout_ref,
                fg_ref, wq_ref, sk_ref, x1_ref, h2t_ref, st_ref):
    w = HG_WIDTH
    lane = lax.broadcasted_iota(jnp.int32, (1, w), 1)
    head_masks = [(lane // HG_DIM) == h for h in range(HG_HEADS)]
    ones = ones_ref[...]

    o = of_ref[...] + ob_ref[...]
    o2 = o * o
    o2_hi = o2.astype(BF16)
    o2_lo = (o2 - o2_hi.astype(F32)).astype(BF16)
    ms = (_dot(o2_hi, ones) + _dot(o2_lo, ones)) * (1.0 / HG_DIM)
    g = gate_ref[...]
    hg_out = o * lax.rsqrt(ms + RMS_EPS) * hgn_ref[...] * (g * jax.nn.sigmoid(g))

    mq = mq_ref[...] * (MEM_WIDTH // MEM_HEADS) ** -0.5
    mk = kv_ref[:, :MEM_WIDTH]
    mv = kv_ref[:, MEM_WIDTH:]
    zero = jnp.zeros_like(mq)
    mem_out = jnp.zeros(mq.shape, F32)
    for hm in head_masks:
        s = _dot_nt(jnp.where(hm, mq, zero), mk)
        e = jnp.exp(s - jnp.max(s, axis=-1, keepdims=True))
        r = _dot(e.astype(BF16), mv) * (1.0 / jnp.sum(e, axis=-1, keepdims=True))
        mem_out = mem_out + jnp.where(hm, r, 0.0)

    nw = NA_WIDTH
    x1 = (x_ref[...] + _dot(na_ref[...], wout_ref[:nw])
          + _dot(hg_out.astype(BF16), wout_ref[nw:nw + w])
          + _dot(mem_out.astype(BF16), wout_ref[nw + w:]))
    x1_ref[...] = x1
    h2f = x1 * _rms_scale(x1) * fg_ref[...]
    h2t_ref[...] = h2f.T.astype(BF16)
    qp = _dot(h2f.astype(BF16), wq_ref[...]).astype(BF16)
    for hc in range(2 * PEER_HEADS):
        st_ref[hc] = _dot_nt(sk_ref[hc], qp[:, hc * PEER_NKEYS:(hc + 1) * PEER_NKEYS])


def _mix(na_out, o_f, o_b, hg, mq, kv, x2, hgn_g, wout_bf, ffn_g, wq_bf, sk_bf, *, seq, tm):
    n, d = x2.shape
    w = HG_WIDTH
    per_batch = seq // tm
    mem_tokens = kv.shape[0] // (n // seq)
    _, blockdiag = _hg_constants()
    blockones = jnp.asarray(blockdiag, BF16)
    nhc = 2 * PEER_HEADS

    def row(width, col=0):
        return pl.BlockSpec((tm, width), lambda i: (i, col))

    def full(a):
        return pl.BlockSpec(a.shape, lambda i: (0,) * a.ndim)

    return pl.pallas_call(
        _mix_kernel,
        grid=(n // tm,),
        in_specs=[row(NA_WIDTH), row(w), row(w), row(w, 4), row(MEM_WIDTH),
                  pl.BlockSpec((mem_tokens, kv.shape[1]), lambda i: (i // per_batch, 0)),
                  row(d), full(hgn_g), full(blockones), full(wout_bf), full(ffn_g), full(wq_bf), full(sk_bf)],
        out_specs=[row(d), pl.BlockSpec((d, tm), lambda i: (0, i)),
                   pl.BlockSpec((nhc, PEER_NKEYS, tm), lambda i: (0, 0, i))],
        out_shape=[jax.ShapeDtypeStruct((n, d), F32), jax.ShapeDtypeStruct((d, n), BF16),
                   jax.ShapeDtypeStruct((nhc, PEER_NKEYS, n), F32)],
        compiler_params=_cparams(("parallel",)),
        name="mix",
    )(na_out, o_f, o_b, hg, mq, kv, x2, hgn_g, blockones, wout_bf, ffn_g, wq_bf, sk_bf)


def _top_values(s, count, want_rank):
    cur = s
    vals = []
    rank = jnp.full(s.shape, float(count), F32) if want_rank else None
    for k in range(count):
        m = jnp.max(cur, axis=0, keepdims=True)
        vals.append(m)
        hit = cur == m
        if want_rank:
            rank = jnp.where(hit, float(k), rank)
        if k + 1 < count:
            cur = jnp.where(hit, -jnp.inf, cur)
    return vals, rank


def _route_kernel(st_ref, r2_ref, e2_ref, l1_ref, e1_ref):
    k = PEER_TOPK
    for h in range(PEER_HEADS):
        s1 = st_ref[2 * h]
        s2 = st_ref[2 * h + 1]
        v1, _ = _top_values(s1, k, False)
        v2, r2 = _top_values(s2, k, True)
        cand = [v1[a] + v2[b] for a in range(k) for b in range(k // (a + 1))]
        pad = (-len(cand)) % 8
        cand = jnp.concatenate(cand + [jnp.full_like(cand[0], -jnp.inf)] * pad, axis=0)
        best, _ = _top_values(cand, k, False)
        tau = best[-1]
        z = best[0] * 0.0
        for c in best:
            z = z + jnp.exp(c - best[0])
        cnt = jnp.zeros(s1.shape, F32)
        for b in range(k):
            cnt = cnt + jnp.where(s1 + v2[b] >= tau, 1.0, 0.0)
        r2_ref[h] = r2.astype(BF16)
        e2_ref[h] = jnp.exp(s2 - v2[0]).astype(BF16)
        l1_ref[h] = cnt
        e1_ref[h] = jnp.exp(s1 - v1[0]) * (1.0 / z)


def _route(st, *, tt):
    nhc, nk, n = st.shape
    heads = nhc // 2
    spec = pl.BlockSpec((heads, nk, tt), lambda i: (0, 0, i))
    return pl.pallas_call(
        _route_kernel,
        grid=(n // tt,),
        in_specs=[pl.BlockSpec((nhc, nk, tt), lambda i: (0, 0, i))],
        out_specs=[spec, spec, spec, spec],
        out_shape=[jax.ShapeDtypeStruct((heads, nk, n), BF16), jax.ShapeDtypeStruct((heads, nk, n), BF16),
                   jax.ShapeDtypeStruct((heads, nk, n), F32), jax.ShapeDtypeStruct((heads, nk, n), F32)],
        compiler_params=_cparams(("parallel",)),
        name="route",
    )(st)


def _gated_activations(act, r2_ref, e2_ref, l1_ref, e1_ref, c):
    nk = PEER_NKEYS
    out = []
    for a in range(act.shape[0] // nk):
        gate = None
        for h in range(PEER_HEADS):
            limit = l1_ref[h, c, a:a + 1, :].astype(BF16)
            weight = e1_ref[h, c, a:a + 1, :].astype(BF16)
            term = jnp.where(r2_ref[h] < limit, e2_ref[h], jnp.zeros((), BF16)) * weight
            gate = term if gate is None else gate + term
        x = act[a * nk:(a + 1) * nk]
        x = 0.5 * x * (1.0 + lax.erf(x * (2.0 ** -0.5)))
        out.append(x.astype(BF16) * gate)
    return jnp.concatenate(out, axis=0)


def _peer_kernel(h2t_ref, u_ref, vt_ref, r2_ref, e2_ref, l1_ref, e1_ref, x1_ref, fg_ref, o_ref, acc, *, ec):
    j = pl.program_id(1)

    @pl.when(j == 0)
    def _():
        acc[...] = jnp.zeros_like(acc)

    total = None
    for c in range(u_ref.shape[0] // ec):
        act = _dot(u_ref[c * ec:(c + 1) * ec], h2t_ref[...])
        w = _gated_activations(act, r2_ref, e2_ref, l1_ref, e1_ref, c)
        part = _dot(vt_ref[:, c * ec:(c + 1) * ec], w)
        total = part if total is None else total + part
    acc[...] += total

    @pl.when(j == pl.num_programs(1) - 1)
    def _():
        y = x1_ref[...] + acc[...].T
        o_ref[...] = y * _rms_scale(y) * fg_ref[...]


def _peer(h2t, u_bf, vt_bf, r2, e2, l1, e1, x1, final_g, *, tn, ec, per_step):
    n, d = x1.shape
    n_exp = u_bf.shape[0]
    groups = ec // PEER_NKEYS
    chunks = n_exp // ec
    heads, nk = PEER_HEADS, PEER_NKEYS
    l1 = l1.reshape(heads, chunks, groups, n)
    e1 = e1.reshape(heads, chunks, groups, n)
    tab = pl.BlockSpec((heads, nk, tn), lambda i, j: (0, 0, i))
    sel = pl.BlockSpec((heads, per_step, groups, tn), lambda i, j: (0, j, 0, i))
    return pl.pallas_call(
        functools.partial(_peer_kernel, ec=ec),
        grid=(n // tn, chunks // per_step),
        in_specs=[pl.BlockSpec((d, tn), lambda i, j: (0, i)),
                  pl.BlockSpec((per_step * ec, d), lambda i, j: (j, 0)),
                  pl.BlockSpec((d, per_step * ec), lambda i, j: (0, j)),
                  tab, tab, sel, sel,
                  pl.BlockSpec((tn, d), lambda i, j: (i, 0)),
                  pl.BlockSpec((1, d), lambda i, j: (0, 0))],
        out_specs=pl.BlockSpec((tn, d), lambda i, j: (i, 0)),
        out_shape=jax.ShapeDtypeStruct((n, d), F32),
        scratch_shapes=[pltpu.VMEM((d, tn), F32)],
        compiler_params=_cparams(("parallel", "arbitrary")),
        name="peer",
    )(h2t, u_bf, vt_bf, r2, e2, l1, e1, x1, final_g)


def kernel(x, mem, attn_norm_g, w_in, na_rpb, hg_lower_bound, hg_out_norm_g, mem_norm_g, w_mem_kv, w_out,
           ffn_norm_g, peer_w_query, peer_sub_keys, peer_u, peer_v, final_norm_g):
    batch, seq, d = x.shape
    n = batch * seq
    x2 = x.reshape(n, d)
    lb_table = jnp.cumsum(jax.nn.softmax(hg_lower_bound.astype(F32), axis=1), axis=1)
    qkv, hg, mq = _in_proj(x2, attn_norm_g[0].reshape(1, d), w_in[0].astype(BF16), tm=512)
    na_out = _na(qkv, _na_bias_table(na_rpb[0]), batch=batch, n_rows=seq // GRID_W)
    o_f, o_b = _hgrn2(hg, lb_table[:, 0], batch=batch, seq=seq, chunks=4)
    kv = _mem_kv(mem.reshape(-1, d), mem_norm_g[0].reshape(1, d), w_mem_kv[0].astype(BF16), batch=batch)
    sk = peer_sub_keys[0].reshape(2 * PEER_HEADS, PEER_NKEYS, -1).astype(BF16)
    x1, h2t, st = _mix(na_out, o_f, o_b, hg, mq, kv, x2, hg_out_norm_g[0].reshape(1, -1), w_out[0].astype(BF16),
                      ffn_norm_g[0].reshape(1, d), peer_w_query[0].astype(BF16), sk, seq=seq, tm=512)
    r2, e2, l1, e1 = _route(st, tt=256)
    out = _peer(h2t, peer_u[0].astype(BF16), peer_v[0].T.astype(BF16), r2, e2, l1, e1, x1,
                final_norm_g.reshape(1, d), tn=512, ec=1024, per_step=2)
    return out.reshape(batch, seq, d)
```

```python
import functools
import math

import numpy as np
import jax
import jax.numpy as jnp
from jax import lax
from jax.experimental import pallas as pl
from jax.experimental.pallas import tpu as pltpu

F32 = jnp.float32
BF16 = jnp.bfloat16

RMS_EPS = 1e-6
GRID_W = 64
NA_HEADS = 8
NA_HEAD_DIM = 64
NA_WIDTH = NA_HEADS * NA_HEAD_DIM
NA_WIN_ROWS = 8
NA_WIN_COLS = 16
NA_GROUP_ROWS = 8
HG_HEADS = 4
HG_DIM = 64
HG_WIDTH = HG_HEADS * HG_DIM
HG_CHUNK = 64
HG_SUB = 8
MEM_HEADS = 4
MEM_WIDTH = 256
PEER_HEADS = 8
PEER_NKEYS = 128
PEER_TOPK = 16
LANE = 128
NEG_BIG = -1e30
KNOCK = 2.0 ** 100

VMEM_LIMIT = 48 * 1024 * 1024


def _cparams(sem):
    return pltpu.CompilerParams(dimension_semantics=sem, vmem_limit_bytes=VMEM_LIMIT)


def _rms_scale(x):
    return lax.rsqrt(jnp.mean(x * x, axis=-1, keepdims=True) + RMS_EPS)


def _dot(a, b):
    return jnp.dot(a, b, preferred_element_type=F32)


def _dot_nt(a, b):
    return lax.dot_general(a, b, (((1,), (1,)), ((), ())), preferred_element_type=F32)


def _dot_tn(a, b):
    return lax.dot_general(a, b, (((0,), (0,)), ((), ())), preferred_element_type=F32)


def _in_proj_kernel(x_ref, g_ref, w_ref, qkv_ref, hg_ref, mq_ref):
    x = x_ref[...]
    h = (x * _rms_scale(x) * g_ref[...]).astype(BF16)
    nq = qkv_ref.shape[1]
    nh = hg_ref.shape[1]
    qkv_ref[...] = _dot(h, w_ref[:, :nq]).astype(BF16)
    hg_ref[...] = _dot(h, w_ref[:, nq:nq + nh])
    mq_ref[...] = _dot(h, w_ref[:, nq + nh:]).astype(BF16)


def _in_proj(x2, g, w_bf, *, tm):
    n, d = x2.shape
    nq, nh, nm = 3 * NA_WIDTH, 5 * HG_WIDTH, MEM_WIDTH
    return pl.pallas_call(
        _in_proj_kernel,
        grid=(n // tm,),
        in_specs=[
            pl.BlockSpec((tm, d), lambda i: (i, 0)),
            pl.BlockSpec((1, d), lambda i: (0, 0)),
            pl.BlockSpec((d, nq + nh + nm), lambda i: (0, 0)),
        ],
        out_specs=[
            pl.BlockSpec((tm, nq), lambda i: (i, 0)),
            pl.BlockSpec((tm, nh), lambda i: (i, 0)),
            pl.BlockSpec((tm, nm), lambda i: (i, 0)),
        ],
        out_shape=[
            jax.ShapeDtypeStruct((n, nq), BF16),
            jax.ShapeDtypeStruct((n, nh), F32),
            jax.ShapeDtypeStruct((n, nm), BF16),
        ],
        compiler_params=_cparams(("parallel",)),
        name="in_proj",
    )(x2, g, w_bf)


def _na_bias_table(rpb):
    cols = np.arange(GRID_W)
    c0 = np.clip(cols - NA_WIN_COLS // 2, 0, GRID_W - NA_WIN_COLS)
    kc = np.arange(GRID_W)
    valid = (kc[None, :] >= c0[:, None]) & (kc[None, :] < c0[:, None] + NA_WIN_COLS)
    dc = kc[None, :] - cols[:, None] + (NA_WIN_COLS - 1)
    onehot = (dc[None] == np.arange(2 * NA_WIN_COLS - 1)[:, None, None]) & valid[None]
    band = jnp.einsum('hrd,dqk->hrqk', rpb.astype(F32), jnp.asarray(onehot, F32),
                      precision=lax.Precision.HIGHEST)
    band = band + jnp.asarray(np.where(valid, 0.0, NEG_BIG), F32)
    t = jnp.concatenate([band[:, :-1], band[:, 1:]], axis=-1)
    n_e = t.shape[1]
    t = t.reshape(NA_HEADS // 2, 2, n_e, GRID_W, 2 * GRID_W).transpose(0, 2, 1, 3, 4)
    return t.reshape(NA_HEADS // 2, n_e, 2 * GRID_W, 2 * GRID_W)


def _na_kernel(q_ref, kp_ref, kc_ref, kn_ref, vp_ref, vc_ref, vn_ref, bias_ref, o_ref, k_scr, v_scr,
               *, n_rows):
    g = pl.program_id(1)
    gt = NA_GROUP_ROWS * GRID_W
    k_scr[0:gt] = kp_ref[...]
    k_scr[gt:2 * gt] = kc_ref[...]
    k_scr[2 * gt:3 * gt] = kn_ref[...]
    v_scr[0:gt] = vp_ref[...]
    v_scr[gt:2 * gt] = vc_ref[...]
    v_scr[2 * gt:3 * gt] = vn_ref[...]
    lane = lax.broadcasted_iota(jnp.int32, (GRID_W, LANE), 1)
    first_head = lane < NA_HEAD_DIM
    scale = NA_HEAD_DIM ** -0.5

    def row_body(j, carry):
        r = g * NA_GROUP_ROWS + j
        rs = jnp.clip(r - NA_WIN_ROWS // 2, 0, n_rows - NA_WIN_ROWS)
        off = pl.multiple_of((rs - g * NA_GROUP_ROWS + NA_GROUP_ROWS) * GRID_W, GRID_W)
        cls = r - rs
        qoff = pl.multiple_of(j * GRID_W, GRID_W)
        for p in range(NA_HEADS // 2):
            cs = slice(p * LANE, (p + 1) * LANE)
            qp = q_ref[pl.ds(qoff, GRID_W), cs] * scale
            zero = jnp.zeros_like(qp)
            qm = jnp.concatenate([jnp.where(first_head, qp, zero), jnp.where(first_head, zero, qp)], axis=0)
            kw = k_scr[pl.ds(off, NA_WIN_ROWS * GRID_W), cs]
            bias = jnp.concatenate(
                [bias_ref[p, NA_WIN_ROWS - 1 - cls + 2 * m] for m in range(NA_WIN_ROWS // 2)], axis=1)
            s = _dot_nt(qm, kw) + bias
            m = jnp.max(s, axis=-1, keepdims=True)
            e = jnp.exp(s - m)
            l = jnp.sum(e, axis=-1, keepdims=True)
            vw = v_scr[pl.ds(off, NA_WIN_ROWS * GRID_W), cs]
            o = _dot(e.astype(BF16), vw) * (1.0 / l)
            o_ref[pl.ds(qoff, GRID_W), cs] = jnp.where(first_head, o[:GRID_W], o[GRID_W:]).astype(o_ref.dtype)
        return carry

    lax.fori_loop(0, NA_GROUP_ROWS, row_body, 0)


def _na(qkv, bias, *, batch, n_rows):
    n = qkv.shape[0]
    gt = NA_GROUP_ROWS * GRID_W
    ng = n_rows // NA_GROUP_ROWS
    w = NA_WIDTH

    def spec(col, shift):
        def imap(b, g):
            return (b * ng + jnp.clip(g + shift, 0, ng - 1), col)
        return pl.BlockSpec((gt, w), imap)

    return pl.pallas_call(
        functools.partial(_na_kernel, n_rows=n_rows),
        grid=(batch, ng),
        in_specs=[spec(0, 0), spec(1, -1), spec(1, 0), spec(1, 1), spec(2, -1), spec(2, 0), spec(2, 1),
                  pl.BlockSpec(bias.shape, lambda b, g: (0, 0, 0, 0))],
        out_specs=pl.BlockSpec((gt, w), lambda b, g: (b * ng + g, 0)),
        out_shape=jax.ShapeDtypeStruct((n, w), BF16),
        scratch_shapes=[pltpu.VMEM((3 * gt, w), BF16), pltpu.VMEM((3 * gt, w), BF16)],
        compiler_params=_cparams(("parallel", "parallel")),
        name="na",
    )(qkv, qkv, qkv, qkv, qkv, qkv, qkv, bias)


def _hg_constants():
    c, sub = HG_CHUNK, HG_SUB
    t = np.arange(c)[:, None]
    s = np.arange(c)[None, :]
    i8 = np.arange(sub)[:, None]
    mats = []
    for rev in (False, True):
        if not rev:
            inc = s <= t
            rb = s < sub * (t // sub)
            eb = s <= sub * (t // sub) + sub - 1
            r8 = s < sub * i8
        else:
            inc = s >= t
            rb = s > sub * (t // sub) + sub - 1
            eb = s >= sub * (t // sub)
            r8 = s > sub * i8 + sub - 1
        extra = np.zeros((16, c), bool)
        extra[:8] = r8
        extra[8] = True
        mats.append(np.concatenate([inc, rb, eb, extra], axis=0))
    tmat = np.stack(mats).astype(np.float32)
    lane_head = np.arange(HG_WIDTH) // HG_DIM
    blockdiag = (lane_head[:, None] == lane_head[None, :]).astype(np.float32)
    return tmat, blockdiag


def _hg_chunk(q, fpre, v, lb, st, tmat, blockones, rev):
    c, sub, nsub, w = HG_CHUNK, HG_SUB, HG_CHUNK // HG_SUB, HG_WIDTH
    f = lb + (1.0 - lb) * jax.nn.sigmoid(fpre)
    kk = 1.0 - f
    lf = jnp.log(f)
    lf_hi = lf.astype(BF16)
    lf_lo = (lf - lf_hi.astype(F32)).astype(BF16)
    cs = _dot(tmat, lf_hi) + _dot(tmat, lf_lo)
    b, rb, eb = cs[0:c], cs[c:2 * c], cs[2 * c:3 * c]
    r8 = cs[3 * c:3 * c + nsub]
    tot = cs[3 * c + nsub:3 * c + nsub + 1]

    lane = lax.broadcasted_iota(jnp.int32, (1, w), 1)
    head_masks = [(lane // HG_DIM) == h for h in range(HG_HEADS)]
    col_sub = (lane % HG_DIM) // sub

    qt = q * jnp.exp(b - rb)
    kt = kk * jnp.exp(eb - b)

    blocks, pairs = [], []
    for i in range(nsub):
        js = range(i) if not rev else range(i + 1, nsub)
        if len(js) == 0:
            continue
        mid = jnp.exp(r8[i:i + 1] - r8)
        for j in js:
            m = j + 1 if not rev else j - 1
            blocks.append(qt[sub * i:sub * (i + 1)] * mid[m:m + 1])
            pairs.append((i, j))
    qbig = jnp.concatenate(blocks, axis=0).astype(BF16)
    zero = jnp.zeros_like(kt)
    kstack = jnp.concatenate([jnp.where(hm, kt, zero) for hm in head_masks], axis=0).astype(BF16)
    roff = _dot_nt(qbig, kstack)
    rows = [jnp.zeros((sub, w), F32)] * nsub
    for n, (i, j) in enumerate(pairs):
        rows[i] = rows[i] + jnp.where(col_sub == j, roff[sub * n:sub * (n + 1)], 0.0)
    a_off = jnp.concatenate(rows, axis=0).astype(BF16)
    vstack = jnp.concatenate([jnp.where(hm, v, zero) for hm in head_masks], axis=0).astype(BF16)
    o = _dot(a_off, vstack)

    row_in_sub = lax.broadcasted_iota(jnp.int32, (nsub, sub, w), 1)
    b3, kk3, v3 = b.reshape(nsub, sub, w), kk.reshape(nsub, sub, w), v.reshape(nsub, sub, w)
    q3 = q.reshape(nsub, sub, w)
    dparts, vparts = [q3 * kk3], [v3]
    for d in range(1, sub):
        shift = d if not rev else sub - d
        valid = (row_in_sub >= d) if not rev else (row_in_sub + d < sub)
        bd = pltpu.roll(b3, shift, axis=1)
        kd = pltpu.roll(kk3, shift, axis=1)
        vparts.append(pltpu.roll(v3, shift, axis=1))
        ex = jnp.where(valid, b3 - bd, 0.0)
        dparts.append(jnp.where(valid, q3 * kd * jnp.exp(ex), 0.0))
    dstack = jnp.concatenate(dparts, axis=0).reshape(sub * c, w).astype(BF16)
    abc = _dot(dstack, blockones).reshape(sub, nsub, sub, w)
    for d in range(sub):
        o = o + (abc[d] * vparts[d]).reshape(c, w)

    o = o + _dot_nt((q * jnp.exp(b)).astype(BF16), st.astype(BF16))
    kdec = (kk * jnp.exp(tot - b)).astype(BF16)
    ut = _dot_tn(v.astype(BF16), kdec)
    st_new = st * jnp.exp(tot) + ut * blockones.astype(F32)
    return o, st_new


def _hg_kernel(qf_ref, ff_ref, vf_ref, qb_ref, fb_ref, vb_ref, lb_ref, tmat_ref, ones_ref,
               of_ref, ob_ref, sf_ref, sb_ref, *, chunks):
    @pl.when(pl.program_id(1) == 0)
    def _():
        sf_ref[...] = jnp.zeros_like(sf_ref)
        sb_ref[...] = jnp.zeros_like(sb_ref)

    blockones = ones_ref[...]
    lbf, lbb = lb_ref[0:1], lb_ref[1:2]

    def body(ci, carry):
        rf = pl.ds(pl.multiple_of(ci * HG_CHUNK, HG_CHUNK), HG_CHUNK)
        rb = pl.ds(pl.multiple_of((chunks - 1 - ci) * HG_CHUNK, HG_CHUNK), HG_CHUNK)
        o_f, s_f = _hg_chunk(qf_ref[rf], ff_ref[rf], vf_ref[rf], lbf, sf_ref[...], tmat_ref[0], blockones, False)
        of_ref[rf] = o_f
        sf_ref[...] = s_f
        o_b, s_b = _hg_chunk(qb_ref[rb], fb_ref[rb], vb_ref[rb], lbb, sb_ref[...], tmat_ref[1], blockones, True)
        ob_ref[rb] = o_b
        sb_ref[...] = s_b
        return carry

    lax.fori_loop(0, chunks, body, 0)


def _hgrn2(hg, lb2, *, batch, seq, chunks):
    n = hg.shape[0]
    tb = chunks * HG_CHUNK
    nb = seq // tb
    w = HG_WIDTH
    tmat, blockdiag = _hg_constants()
    tmat = jnp.asarray(tmat, BF16)
    blockones = jnp.asarray(blockdiag, BF16)

    def fwd(col):
        return pl.BlockSpec((tb, w), lambda b, i: (b * nb + i, col))

    def bwd(col):
        return pl.BlockSpec((tb, w), lambda b, i: (b * nb + nb - 1 - i, col))

    return pl.pallas_call(
        functools.partial(_hg_kernel, chunks=chunks),
        grid=(batch, nb),
        in_specs=[fwd(0), fwd(1), fwd(3), bwd(0), bwd(2), bwd(3),
                  pl.BlockSpec((2, w), lambda b, i: (0, 0)),
                  pl.BlockSpec(tmat.shape, lambda b, i: (0, 0, 0)),
                  pl.BlockSpec((w, w), lambda b, i: (0, 0))],
        out_specs=[fwd(0), bwd(0)],
        out_shape=[jax.ShapeDtypeStruct((n, w), F32), jax.ShapeDtypeStruct((n, w), F32)],
        scratch_shapes=[pltpu.VMEM((w, w), F32), pltpu.VMEM((w, w), F32)],
        compiler_params=_cparams(("parallel", "arbitrary")),
        name="hgrn2",
    )(hg, hg, hg, hg, hg, hg, lb2, tmat, blockones)


def _mem_kv_kernel(m_ref, g_ref, w_ref, o_ref):
    m = m_ref[...]
    h = (m * _rms_scale(m) * g_ref[...]).astype(BF16)
    o_ref[...] = _dot(h, w_ref[...]).astype(BF16)


def _mem_kv(mem2, g, w_bf, *, batch):
    n, d = mem2.shape
    m = n // batch
    return pl.pallas_call(
        _mem_kv_kernel,
        grid=(batch,),
        in_specs=[pl.BlockSpec((m, d), lambda b: (b, 0)),
                  pl.BlockSpec((1, d), lambda b: (0, 0)),
                  pl.BlockSpec(w_bf.shape, lambda b: (0, 0))],
        out_specs=pl.BlockSpec((m, w_bf.shape[1]), lambda b: (b, 0)),
        out_shape=jax.ShapeDtypeStruct((n, w_bf.shape[1]), BF16),
        compiler_params=_cparams(("parallel",)),
        name="mem_kv",
    )(mem2, g, w_bf)


def _mix_kernel(na_ref, of_ref, ob_ref, gate_ref, mq_ref, kv_ref, x_ref, hgn_ref, ones_ref, wout_ref,
                fg_ref, wq_ref, sk_ref, x1_ref, h2t_ref, st_ref):
    w = HG_WIDTH
    lane = lax.broadcasted_iota(jnp.int32, (1, w), 1)
    head_masks = [(lane // HG_DIM) == h for h in range(HG_HEADS)]
    ones = ones_ref[...]

    o = of_ref[...] + ob_ref[...]
    o2 = o * o
    o2_hi = o2.astype(BF16)
    o2_lo = (o2 - o2_hi.astype(F32)).astype(BF16)
    ms = (_dot(o2_hi, ones) + _dot(o2_lo, ones)) * (1.0 / HG_DIM)
    g = gate_ref[...]
    hg_out = o * lax.rsqrt(ms + RMS_EPS) * hgn_ref[...] * (g * jax.nn.sigmoid(g))

    mq = mq_ref[...] * (MEM_WIDTH // MEM_HEADS) ** -0.5
    mk = kv_ref[:, :MEM_WIDTH]
    mv = kv_ref[:, MEM_WIDTH:]
    zero = jnp.zeros_like(mq)
    mem_out = jnp.zeros(mq.shape, F32)
    for hm in head_masks:
        s = _dot_nt(jnp.where(hm, mq, zero), mk)
        e = jnp.exp(s - jnp.max(s, axis=-1, keepdims=True))
        r = _dot(e.astype(BF16), mv) * (1.0 / jnp.sum(e, axis=-1, keepdims=True))
        mem_out = mem_out + jnp.where(hm, r, 0.0)

    nw = NA_WIDTH
    x1 = (x_ref[...] + _dot(na_ref[...], wout_ref[:nw])
          + _dot(hg_out.astype(BF16), wout_ref[nw:nw + w])
          + _dot(mem_out.astype(BF16), wout_ref[nw + w:]))
    x1_ref[...] = x1
    h2f = x1 * _rms_scale(x1) * fg_ref[...]
    h2t_ref[...] = h2f.T.astype(BF16)
    qp = _dot(h2f.astype(BF16), wq_ref[...]).astype(BF16)
    for hc in range(2 * PEER_HEADS):
        st_ref[hc] = _dot_nt(sk_ref[hc], qp[:, hc * PEER_NKEYS:(hc + 1) * PEER_NKEYS])


def _mix(na_out, o_f, o_b, hg, mq, kv, x2, hgn_g, wout_bf, ffn_g, wq_bf, sk_bf, *, seq, tm):
    n, d = x2.shape
    w = HG_WIDTH
    per_batch = seq // tm
    mem_tokens = kv.shape[0] // (n // seq)
    _, blockdiag = _hg_constants()
    blockones = jnp.asarray(blockdiag, BF16)
    nhc = 2 * PEER_HEADS

    def row(width, col=0):
        return pl.BlockSpec((tm, width), lambda i: (i, col))

    def full(a):
        return pl.BlockSpec(a.shape, lambda i: (0,) * a.ndim)

    return pl.pallas_call(
        _mix_kernel,
        grid=(n // tm,),
        in_specs=[row(NA_WIDTH), row(w), row(w), row(w, 4), row(MEM_WIDTH),
                  pl.BlockSpec((mem_tokens, kv.shape[1]), lambda i: (i // per_batch, 0)),
                  row(d), full(hgn_g), full(blockones), full(wout_bf), full(ffn_g), full(wq_bf), full(sk_bf)],
        out_specs=[row(d), pl.BlockSpec((d, tm), lambda i: (0, i)),
                   pl.BlockSpec((nhc, PEER_NKEYS, tm), lambda i: (0, 0, i))],
        out_shape=[jax.ShapeDtypeStruct((n, d), F32), jax.ShapeDtypeStruct((d, n), BF16),
                   jax.ShapeDtypeStruct((nhc, PEER_NKEYS, n), F32)],
        compiler_params=_cparams(("parallel",)),
        name="mix",
    )(na_out, o_f, o_b, hg, mq, kv, x2, hgn_g, blockones, wout_bf, ffn_g, wq_bf, sk_bf)


def _top_values(s, count, want_rank):
    cur = s
    vals = []
    for k in range(count):
        m = jnp.max(cur, axis=0, keepdims=True)
        vals.append(m)
        if k + 1 < count or want_rank:
            cur = jnp.where(cur >= m, -(k + 1) * KNOCK, cur)
    if not want_rank:
        return vals, None
    rank = jnp.where(cur <= -KNOCK, cur * (-1.0 / KNOCK) - 1.0, float(count))
    return vals, rank


def _route_kernel(st_ref, r2_ref, e2_ref, l1_ref, e1_ref):
    k = PEER_TOPK
    for h in range(PEER_HEADS):
        s1 = st_ref[2 * h]
        s2 = st_ref[2 * h + 1]
        v1, _ = _top_values(s1, k + 1, False)
        v2, r2 = _top_values(s2, k + 1, True)
        cand = [v1[a] + v2[b] for a in range(k + 1) for b in range((k + 1) // (a + 1))]
        pad = (-len(cand)) % 8
        cand = jnp.concatenate(cand + [jnp.full_like(cand[0], -jnp.inf)] * pad, axis=0)
        best, _ = _top_values(cand, k + 1, False)
        tau = 0.5 * (best[k - 1] + best[k])
        z = jnp.zeros_like(best[0])
        for c in best[:k]:
            z = z + jnp.exp(c - best[0])
        cnt = jnp.zeros(s1.shape, F32)
        for b in range(k):
            cnt = cnt + jnp.where(s1 >= tau - v2[b], 1.0, 0.0)
        r2_ref[h] = r2.astype(BF16)
        e2_ref[h] = jnp.exp(s2 - v2[0]).astype(BF16)
        l1_ref[h] = cnt
        e1_ref[h] = jnp.exp(s1 - v1[0]) * (1.0 / z)


def _route(st, *, tt):
    nhc, nk, n = st.shape
    heads = nhc // 2
    spec = pl.BlockSpec((heads, nk, tt), lambda i: (0, 0, i))
    return pl.pallas_call(
        _route_kernel,
        grid=(n // tt,),
        in_specs=[pl.BlockSpec((nhc, nk, tt), lambda i: (0, 0, i))],
        out_specs=[spec, spec, spec, spec],
        out_shape=[jax.ShapeDtypeStruct((heads, nk, n), BF16), jax.ShapeDtypeStruct((heads, nk, n), BF16),
                   jax.ShapeDtypeStruct((heads, nk, n), F32), jax.ShapeDtypeStruct((heads, nk, n), F32)],
        compiler_params=_cparams(("parallel",)),
        name="route",
    )(st)


def _gated_activations(act, r2_ref, e2_ref, l1_ref, e1_ref, c):
    nk = PEER_NKEYS
    out = []
    for a in range(act.shape[0] // nk):
        gate = None
        for h in range(PEER_HEADS):
            limit = l1_ref[h, c, a:a + 1, :].astype(BF16)
            weight = e1_ref[h, c, a:a + 1, :].astype(BF16)
            term = jnp.where(r2_ref[h] < limit, e2_ref[h], jnp.zeros((), BF16)) * weight
            gate = term if gate is None else gate + term
        x = act[a * nk:(a + 1) * nk]
        x = 0.5 * x * (1.0 + lax.erf(x * (2.0 ** -0.5)))
        out.append(x.astype(BF16) * gate)
    return jnp.concatenate(out, axis=0)


def _peer_kernel(h2t_ref, u_ref, vt_ref, r2_ref, e2_ref, l1_ref, e1_ref, x1_ref, fg_ref, o_ref, acc, *, ec):
    j = pl.program_id(1)

    @pl.when(j == 0)
    def _():
        acc[...] = jnp.zeros_like(acc)

    total = None
    for c in range(u_ref.shape[0] // ec):
        act = _dot(u_ref[c * ec:(c + 1) * ec], h2t_ref[...])
        w = _gated_activations(act, r2_ref, e2_ref, l1_ref, e1_ref, c)
        part = _dot(vt_ref[:, c * ec:(c + 1) * ec], w)
        total = part if total is None else total + part
    acc[...] += total

    @pl.when(j == pl.num_programs(1) - 1)
    def _():
        y = x1_ref[...] + acc[...].T
        o_ref[...] = y * _rms_scale(y) * fg_ref[...]


def _peer(h2t, u_bf, vt_bf, r2, e2, l1, e1, x1, final_g, *, tn, ec, per_step):
    n, d = x1.shape
    n_exp = u_bf.shape[0]
    groups = ec // PEER_NKEYS
    chunks = n_exp // ec
    heads, nk = PEER_HEADS, PEER_NKEYS
    l1 = l1.reshape(heads, chunks, groups, n)
    e1 = e1.reshape(heads, chunks, groups, n)
    tab = pl.BlockSpec((heads, nk, tn), lambda i, j: (0, 0, i))
    sel = pl.BlockSpec((heads, per_step, groups, tn), lambda i, j: (0, j, 0, i))
    return pl.pallas_call(
        functools.partial(_peer_kernel, ec=ec),
        grid=(n // tn, chunks // per_step),
        in_specs=[pl.BlockSpec((d, tn), lambda i, j: (0, i)),
                  pl.BlockSpec((per_step * ec, d), lambda i, j: (j, 0)),
                  pl.BlockSpec((d, per_step * ec), lambda i, j: (0, j)),
                  tab, tab, sel, sel,
                  pl.BlockSpec((tn, d), lambda i, j: (i, 0)),
                  pl.BlockSpec((1, d), lambda i, j: (0, 0))],
        out_specs=pl.BlockSpec((tn, d), lambda i, j: (i, 0)),
        out_shape=jax.ShapeDtypeStruct((n, d), F32),
        scratch_shapes=[pltpu.VMEM((d, tn), F32)],
        compiler_params=_cparams(("parallel", "arbitrary")),
        name="peer",
    )(h2t, u_bf, vt_bf, r2, e2, l1, e1, x1, final_g)


def kernel(x, mem, attn_norm_g, w_in, na_rpb, hg_lower_bound, hg_out_norm_g, mem_norm_g, w_mem_kv, w_out,
           ffn_norm_g, peer_w_query, peer_sub_keys, peer_u, peer_v, final_norm_g):
    batch, seq, d = x.shape
    n = batch * seq
    x2 = x.reshape(n, d)
    lb_table = jnp.cumsum(jax.nn.softmax(hg_lower_bound.astype(F32), axis=1), axis=1)
    qkv, hg, mq = _in_proj(x2, attn_norm_g[0].reshape(1, d), w_in[0].astype(BF16), tm=512)
    na_out = _na(qkv, _na_bias_table(na_rpb[0]), batch=batch, n_rows=seq // GRID_W)
    o_f, o_b = _hgrn2(hg, lb_table[:, 0], batch=batch, seq=seq, chunks=8)
    kv = _mem_kv(mem.reshape(-1, d), mem_norm_g[0].reshape(1, d), w_mem_kv[0].astype(BF16), batch=batch)
    sk = peer_sub_keys[0].reshape(2 * PEER_HEADS, PEER_NKEYS, -1).astype(BF16)
    x1, h2t, st = _mix(na_out, o_f, o_b, hg, mq, kv, x2, hg_out_norm_g[0].reshape(1, -1), w_out[0].astype(BF16),
                      ffn_norm_g[0].reshape(1, d), peer_w_query[0].astype(BF16), sk, seq=seq, tm=512)
    r2, e2, l1, e1 = _route(st, tt=256)
    out = _peer(h2t, peer_u[0].astype(BF16), peer_v[0].T.astype(BF16), r2, e2, l1, e1, x1,
                final_norm_g.reshape(1, d), tn=512, ec=1024, per_step=2)
    return out.reshape(batch, seq, d)
```

```python
import functools
import math

import numpy as np
import jax
import jax.numpy as jnp
from jax import lax
from jax.experimental import pallas as pl
from jax.experimental.pallas import tpu as pltpu

F32 = jnp.float32
BF16 = jnp.bfloat16

RMS_EPS = 1e-6
GRID_W = 64
NA_HEADS = 8
NA_HEAD_DIM = 64
NA_WIDTH = NA_HEADS * NA_HEAD_DIM
NA_WIN_ROWS = 8
NA_WIN_COLS = 16
NA_GROUP_ROWS = 8
NA_QUAD_ROWS = 4
NA_KEY_ROWS = NA_QUAD_ROWS + NA_WIN_ROWS
NA_TABLE_ROWS = 2 * NA_WIN_ROWS - 1 + NA_QUAD_ROWS
HG_HEADS = 4
HG_DIM = 64
HG_WIDTH = HG_HEADS * HG_DIM
HG_CHUNK = 64
HG_SUB = 8
MEM_HEADS = 4
MEM_WIDTH = 256
PEER_HEADS = 8
PEER_NKEYS = 128
PEER_TOPK = 16
LANE = 128
BF16_ROWS = 16
NEG_BIG = -1e30
KNOCK = 2.0 ** 100

VMEM_LIMIT = 48 * 1024 * 1024


def _cparams(sem):
    return pltpu.CompilerParams(dimension_semantics=sem, vmem_limit_bytes=VMEM_LIMIT)


def _rms_scale(x):
    return lax.rsqrt(jnp.mean(x * x, axis=-1, keepdims=True) + RMS_EPS)


def _dot(a, b):
    return jnp.dot(a, b, preferred_element_type=F32)


def _dot_nt(a, b):
    return lax.dot_general(a, b, (((1,), (1,)), ((), ())), preferred_element_type=F32)


def _dot_tn(a, b):
    return lax.dot_general(a, b, (((0,), (0,)), ((), ())), preferred_element_type=F32)


def _in_proj_kernel(x_ref, g_ref, w_ref, wvt_ref, qk_ref, vt_ref, hg_ref, mq_ref):
    x = x_ref[...]
    h = (x * _rms_scale(x) * g_ref[...]).astype(BF16)
    nqk = qk_ref.shape[1]
    nv = vt_ref.shape[0]
    nh = hg_ref.shape[1]
    qk_ref[...] = _dot(h, w_ref[:, :nqk]).astype(BF16)
    vt_ref[...] = _dot_nt(wvt_ref[...], h).astype(BF16)
    hg_ref[...] = _dot(h, w_ref[:, nqk + nv:nqk + nv + nh])
    mq_ref[...] = _dot(h, w_ref[:, nqk + nv + nh:]).astype(BF16)


def _in_proj(x2, g, w_bf, *, tm):
    n, d = x2.shape
    nqk, nv, nh, nm = 2 * NA_WIDTH, NA_WIDTH, 5 * HG_WIDTH, MEM_WIDTH
    wvt = w_bf[:, nqk:nqk + nv].T
    return pl.pallas_call(
        _in_proj_kernel,
        grid=(n // tm,),
        in_specs=[
            pl.BlockSpec((tm, d), lambda i: (i, 0)),
            pl.BlockSpec((1, d), lambda i: (0, 0)),
            pl.BlockSpec(w_bf.shape, lambda i: (0, 0)),
            pl.BlockSpec((nv, d), lambda i: (0, 0)),
        ],
        out_specs=[
            pl.BlockSpec((tm, nqk), lambda i: (i, 0)),
            pl.BlockSpec((nv, tm), lambda i: (0, i)),
            pl.BlockSpec((tm, nh), lambda i: (i, 0)),
            pl.BlockSpec((tm, nm), lambda i: (i, 0)),
        ],
        out_shape=[
            jax.ShapeDtypeStruct((n, nqk), BF16),
            jax.ShapeDtypeStruct((nv, n), BF16),
            jax.ShapeDtypeStruct((n, nh), F32),
            jax.ShapeDtypeStruct((n, nm), BF16),
        ],
        compiler_params=_cparams(("parallel",)),
        name="in_proj",
    )(x2, g, w_bf, wvt)


def _na_tables(rpb):
    cols = np.arange(GRID_W)
    c0 = np.clip(cols - NA_WIN_COLS // 2, 0, GRID_W - NA_WIN_COLS)
    kc = np.arange(GRID_W)
    valid = (kc[None, :] >= c0[:, None]) & (kc[None, :] < c0[:, None] + NA_WIN_COLS)
    dc = kc[None, :] - cols[:, None] + (NA_WIN_COLS - 1)
    onehot = (dc[None] == np.arange(2 * NA_WIN_COLS - 1)[:, None, None]) & valid[None]
    band = jnp.einsum('hrd,dqk->hrkq', rpb.astype(F32), jnp.asarray(onehot, F32),
                      precision=lax.Precision.HIGHEST)
    band = band + jnp.asarray(np.where(valid.T, 0.0, NEG_BIG), F32)
    heads, n_dr = band.shape[:2]
    quad, n_e = NA_QUAD_ROWS, NA_TABLE_ROWS
    neg = lambda k: jnp.full((heads, k, GRID_W, GRID_W), NEG_BIG, F32)
    padded = jnp.concatenate([neg(quad - 1), band, neg(n_e - n_dr)], axis=1)
    tab = jnp.concatenate([padded[:, quad - 1 - t:quad - 1 - t + n_e] for t in range(quad)], axis=-1)
    tab = tab.reshape(heads, n_e * GRID_W, quad * GRID_W)
    i = np.arange(NA_KEY_ROWS)[:, None]
    t = np.arange(quad)[None, :]
    inside = np.stack([(i >= t) & (i < t + NA_WIN_ROWS), (i < NA_WIN_ROWS) & (t >= 0)])
    mask = np.where(inside, 0.0, NEG_BIG).astype(np.float32)
    mask = np.broadcast_to(mask[:, :, None, :, None], (2, NA_KEY_ROWS, GRID_W, quad, GRID_W))
    return tab, jnp.asarray(mask.reshape(2, NA_KEY_ROWS * GRID_W, quad * GRID_W))


def _na_kernel(q_ref, kp_ref, kc_ref, kn_ref, vp_ref, vc_ref, vn_ref, tab_ref, mask_ref, o_ref, k_scr, v_scr,
               s_scr, p_scr, inv_scr, *, n_rows):
    g = pl.program_id(1)
    gt = NA_GROUP_ROWS * GRID_W
    for blk, ref in enumerate((kp_ref, kc_ref, kn_ref)):
        k_scr[blk * gt:(blk + 1) * gt] = ref[...]
    pairs_per_block = gt // LANE
    for blk, ref in enumerate((vp_ref, vc_ref, vn_ref)):
        for t in range(pairs_per_block):
            v_scr[blk * pairs_per_block + t] = ref[:, t * LANE:(t + 1) * LANE]
    first_head_lanes = lax.broadcasted_iota(jnp.int32, (1, LANE), 1) < NA_HEAD_DIM
    first_head_rows = lax.broadcasted_iota(jnp.int32, (LANE, 1), 0) < NA_HEAD_DIM
    scale = NA_HEAD_DIM ** -0.5
    qt = NA_QUAD_ROWS * GRID_W
    nkeys = NA_KEY_ROWS * GRID_W

    for quad in range(NA_GROUP_ROWS // NA_QUAD_ROWS):
        r = g * NA_GROUP_ROWS + quad * NA_QUAD_ROWS
        w0 = jnp.clip(r - NA_WIN_ROWS // 2, 0, n_rows - NA_WIN_ROWS)
        rel = w0 - g * NA_GROUP_ROWS + NA_GROUP_ROWS
        koff = pl.multiple_of(rel * GRID_W, LANE)
        pr0 = lax.shift_right_logical(rel, 1)
        toff = pl.multiple_of((w0 - r + NA_WIN_ROWS - 1) * GRID_W, GRID_W)
        edge = jnp.logical_or(r == 0, r == n_rows - NA_QUAD_ROWS)
        cls = jnp.where(edge, 1, 0)

        @pl.when(g >= 0)
        def _scores():
            mask = mask_ref[cls]
            for p in range(NA_HEADS // 2):
                cs = slice(p * LANE, (p + 1) * LANE)
                qp = q_ref[quad * qt:(quad + 1) * qt, cs] * scale
                zero = jnp.zeros_like(qp)
                kw = k_scr[pl.ds(koff, nkeys), cs]
                for hh in range(2):
                    qm = (jnp.where(first_head_lanes, qp, zero) if hh == 0
                          else jnp.where(first_head_lanes, zero, qp))
                    s_scr[2 * p + hh] = _dot_nt(kw, qm) + tab_ref[2 * p + hh, pl.ds(toff, nkeys), :] + mask

        @pl.when(g >= 0)
        def _softmax():
            for h in range(NA_HEADS):
                s = s_scr[h]
                e = jnp.exp(s - jnp.max(s, axis=0, keepdims=True))
                inv = 1.0 / jnp.sum(e, axis=0, keepdims=True)
                inv_scr[h] = jnp.broadcast_to(inv, inv_scr.shape[1:])
                p_scr[h] = e.astype(BF16)

        @pl.when(g >= 0)
        def _values():
            for p in range(NA_HEADS // 2):
                cs = slice(p * LANE, (p + 1) * LANE)
                vwin = jnp.concatenate([v_scr[pr0 + i, cs, :] for i in range(nkeys // LANE)], axis=1)
                outs = [_dot(vwin, p_scr[2 * p + hh]) * inv_scr[2 * p + hh, 0:1] for hh in range(2)]
                o_ref[cs, quad * qt:(quad + 1) * qt] = jnp.where(
                    first_head_rows, outs[0], outs[1]).astype(o_ref.dtype)


def _na(qk, vt, tab, mask, *, batch, n_rows):
    n = qk.shape[0]
    gt = NA_GROUP_ROWS * GRID_W
    ng = n_rows // NA_GROUP_ROWS
    w = NA_WIDTH
    nkeys, nq = NA_KEY_ROWS * GRID_W, NA_QUAD_ROWS * GRID_W

    def blk(g, shift):
        return jnp.clip(g + shift, 0, ng - 1)

    def kspec(shift):
        return pl.BlockSpec((gt, w), lambda b, g: (b * ng + blk(g, shift), 1))

    def vspec(shift):
        return pl.BlockSpec((w, gt), lambda b, g: (0, b * ng + blk(g, shift)))

    return pl.pallas_call(
        functools.partial(_na_kernel, n_rows=n_rows),
        grid=(batch, ng),
        in_specs=[pl.BlockSpec((gt, w), lambda b, g: (b * ng + g, 0)),
                  kspec(-1), kspec(0), kspec(1), vspec(-1), vspec(0), vspec(1),
                  pl.BlockSpec(tab.shape, lambda b, g: (0, 0, 0), pipeline_mode=pl.Buffered(1)),
                  pl.BlockSpec(mask.shape, lambda b, g: (0, 0, 0))],
        out_specs=pl.BlockSpec((w, gt), lambda b, g: (0, b * ng + g)),
        out_shape=jax.ShapeDtypeStruct((w, n), BF16),
        scratch_shapes=[pltpu.VMEM((3 * gt, w), BF16), pltpu.VMEM((3 * gt // LANE, w, LANE), BF16),
                        pltpu.VMEM((NA_HEADS, nkeys, nq), F32), pltpu.VMEM((NA_HEADS, nkeys, nq), BF16),
                        pltpu.VMEM((NA_HEADS, 8, nq), F32)],
        compiler_params=_cparams(("parallel", "parallel")),
        name="na",
    )(qk, qk, qk, qk, vt, vt, vt, tab, mask)


def _hg_constants():
    c, sub = HG_CHUNK, HG_SUB
    t = np.arange(c)[:, None]
    s = np.arange(c)[None, :]
    i8 = np.arange(sub)[:, None]
    mats = []
    for rev in (False, True):
        if not rev:
            inc = s <= t
            rb = s < sub * (t // sub)
            eb = s <= sub * (t // sub) + sub - 1
            r8 = s < sub * i8
        else:
            inc = s >= t
            rb = s > sub * (t // sub) + sub - 1
            eb = s >= sub * (t // sub)
            r8 = s > sub * i8 + sub - 1
        extra = np.zeros((16, c), bool)
        extra[:8] = r8
        extra[8] = True
        mats.append(np.concatenate([inc, rb, eb, extra], axis=0))
    tmat = np.stack(mats).astype(np.float32)
    lane_head = np.arange(HG_WIDTH) // HG_DIM
    blockdiag = (lane_head[:, None] == lane_head[None, :]).astype(np.float32)
    return tmat, blockdiag


def _hg_chunk(q, fpre, v, lb, st, tmat, blockones, rev):
    c, sub, nsub, w = HG_CHUNK, HG_SUB, HG_CHUNK // HG_SUB, HG_WIDTH
    f = lb + (1.0 - lb) * jax.nn.sigmoid(fpre)
    kk = 1.0 - f
    lf = jnp.log(f)
    lf_hi = lf.astype(BF16)
    lf_lo = (lf - lf_hi.astype(F32)).astype(BF16)
    cs = _dot(tmat, lf_hi) + _dot(tmat, lf_lo)
    b, rb, eb = cs[0:c], cs[c:2 * c], cs[2 * c:3 * c]
    r8 = cs[3 * c:3 * c + nsub]
    tot = cs[3 * c + nsub:3 * c + nsub + 1]

    lane = lax.broadcasted_iota(jnp.int32, (1, w), 1)
    head_masks = [(lane // HG_DIM) == h for h in range(HG_HEADS)]
    col_sub = (lane % HG_DIM) // sub

    qt = q * jnp.exp(b - rb)
    kt = kk * jnp.exp(eb - b)

    blocks, pairs = [], []
    for i in range(nsub):
        js = range(i) if not rev else range(i + 1, nsub)
        if len(js) == 0:
            continue
        mid = jnp.exp(r8[i:i + 1] - r8)
        for j in js:
            m = j + 1 if not rev else j - 1
            blocks.append(qt[sub * i:sub * (i + 1)] * mid[m:m + 1])
            pairs.append((i, j))
    qbig = jnp.concatenate(blocks, axis=0).astype(BF16)
    zero = jnp.zeros_like(kt)
    kstack = jnp.concatenate([jnp.where(hm, kt, zero) for hm in head_masks], axis=0).astype(BF16)
    roff = _dot_nt(qbig, kstack)
    rows = [jnp.zeros((sub, w), F32)] * nsub
    for n, (i, j) in enumerate(pairs):
        rows[i] = rows[i] + jnp.where(col_sub == j, roff[sub * n:sub * (n + 1)], 0.0)
    a_off = jnp.concatenate(rows, axis=0).astype(BF16)
    vstack = jnp.concatenate([jnp.where(hm, v, zero) for hm in head_masks], axis=0).astype(BF16)
    o = _dot(a_off, vstack)

    row_in_sub = lax.broadcasted_iota(jnp.int32, (nsub, sub, w), 1)
    b3, kk3, v3 = b.reshape(nsub, sub, w), kk.reshape(nsub, sub, w), v.reshape(nsub, sub, w)
    q3 = q.reshape(nsub, sub, w)
    dparts, vparts = [q3 * kk3], [v3]
    for d in range(1, sub):
        shift = d if not rev else sub - d
        valid = (row_in_sub >= d) if not rev else (row_in_sub + d < sub)
        bd = pltpu.roll(b3, shift, axis=1)
        kd = pltpu.roll(kk3, shift, axis=1)
        vparts.append(pltpu.roll(v3, shift, axis=1))
        ex = jnp.where(valid, b3 - bd, 0.0)
        dparts.append(jnp.where(valid, q3 * kd * jnp.exp(ex), 0.0))
    dstack = jnp.concatenate(dparts, axis=0).reshape(sub * c, w).astype(BF16)
    abc = _dot(dstack, blockones).reshape(sub, nsub, sub, w)
    for d in range(sub):
        o = o + (abc[d] * vparts[d]).reshape(c, w)

    o = o + _dot_nt((q * jnp.exp(b)).astype(BF16), st.astype(BF16))
    kdec = (kk * jnp.exp(tot - b)).astype(BF16)
    ut = _dot_tn(v.astype(BF16), kdec)
    st_new = st * jnp.exp(tot) + ut * blockones.astype(F32)
    return o, st_new


def _hg_kernel(qf_ref, ff_ref, vf_ref, qb_ref, fb_ref, vb_ref, lb_ref, tmat_ref, ones_ref,
               of_ref, ob_ref, sf_ref, sb_ref, *, chunks):
    @pl.when(pl.program_id(1) == 0)
    def _():
        sf_ref[...] = jnp.zeros_like(sf_ref)
        sb_ref[...] = jnp.zeros_like(sb_ref)

    blockones = ones_ref[...]
    lbf, lbb = lb_ref[0:1], lb_ref[1:2]

    def body(ci, carry):
        rf = pl.ds(pl.multiple_of(ci * HG_CHUNK, HG_CHUNK), HG_CHUNK)
        rb = pl.ds(pl.multiple_of((chunks - 1 - ci) * HG_CHUNK, HG_CHUNK), HG_CHUNK)
        o_f, s_f = _hg_chunk(qf_ref[rf], ff_ref[rf], vf_ref[rf], lbf, sf_ref[...], tmat_ref[0], blockones, False)
        of_ref[rf] = o_f
        sf_ref[...] = s_f
        o_b, s_b = _hg_chunk(qb_ref[rb], fb_ref[rb], vb_ref[rb], lbb, sb_ref[...], tmat_ref[1], blockones, True)
        ob_ref[rb] = o_b
        sb_ref[...] = s_b
        return carry

    lax.fori_loop(0, chunks, body, 0)


def _hgrn2(hg, lb2, *, batch, seq, chunks):
    n = hg.shape[0]
    tb = chunks * HG_CHUNK
    nb = seq // tb
    w = HG_WIDTH
    tmat, blockdiag = _hg_constants()
    tmat = jnp.asarray(tmat, BF16)
    blockones = jnp.asarray(blockdiag, BF16)

    def fwd(col):
        return pl.BlockSpec((tb, w), lambda b, i: (b * nb + i, col))

    def bwd(col):
        return pl.BlockSpec((tb, w), lambda b, i: (b * nb + nb - 1 - i, col))

    return pl.pallas_call(
        functools.partial(_hg_kernel, chunks=chunks),
        grid=(batch, nb),
        in_specs=[fwd(0), fwd(1), fwd(3), bwd(0), bwd(2), bwd(3),
                  pl.BlockSpec((2, w), lambda b, i: (0, 0)),
                  pl.BlockSpec(tmat.shape, lambda b, i: (0, 0, 0)),
                  pl.BlockSpec((w, w), lambda b, i: (0, 0))],
        out_specs=[fwd(0), bwd(0)],
        out_shape=[jax.ShapeDtypeStruct((n, w), F32), jax.ShapeDtypeStruct((n, w), F32)],
        scratch_shapes=[pltpu.VMEM((w, w), F32), pltpu.VMEM((w, w), F32)],
        compiler_params=_cparams(("parallel", "arbitrary")),
        name="hgrn2",
    )(hg, hg, hg, hg, hg, hg, lb2, tmat, blockones)


def _mem_kv_kernel(m_ref, g_ref, w_ref, o_ref):
    m = m_ref[...]
    h = (m * _rms_scale(m) * g_ref[...]).astype(BF16)
    o_ref[...] = _dot(h, w_ref[...]).astype(BF16)


def _mem_kv(mem2, g, w_bf, *, batch):
    n, d = mem2.shape
    m = n // batch
    return pl.pallas_call(
        _mem_kv_kernel,
        grid=(batch,),
        in_specs=[pl.BlockSpec((m, d), lambda b: (b, 0)),
                  pl.BlockSpec((1, d), lambda b: (0, 0)),
                  pl.BlockSpec(w_bf.shape, lambda b: (0, 0))],
        out_specs=pl.BlockSpec((m, w_bf.shape[1]), lambda b: (b, 0)),
        out_shape=jax.ShapeDtypeStruct((n, w_bf.shape[1]), BF16),
        compiler_params=_cparams(("parallel",)),
        name="mem_kv",
    )(mem2, g, w_bf)


def _mix_kernel(nat_ref, of_ref, ob_ref, gate_ref, mq_ref, kv_ref, x_ref, hgn_ref, ones_ref, wout_ref,
                fg_ref, wq_ref, sk_ref, x1_ref, h2t_ref, st_ref):
    w = HG_WIDTH
    lane = lax.broadcasted_iota(jnp.int32, (1, w), 1)
    head_masks = [(lane // HG_DIM) == h for h in range(HG_HEADS)]
    ones = ones_ref[...]

    o = of_ref[...] + ob_ref[...]
    o2 = o * o
    o2_hi = o2.astype(BF16)
    o2_lo = (o2 - o2_hi.astype(F32)).astype(BF16)
    ms = (_dot(o2_hi, ones) + _dot(o2_lo, ones)) * (1.0 / HG_DIM)
    g = gate_ref[...]
    hg_out = o * lax.rsqrt(ms + RMS_EPS) * hgn_ref[...] * (g * jax.nn.sigmoid(g))

    mq = mq_ref[...] * (MEM_WIDTH // MEM_HEADS) ** -0.5
    mk = kv_ref[:, :MEM_WIDTH]
    mv = kv_ref[:, MEM_WIDTH:]
    zero = jnp.zeros_like(mq)
    mem_out = jnp.zeros(mq.shape, F32)
    for hm in head_masks:
        s = _dot_nt(jnp.where(hm, mq, zero), mk)
        e = jnp.exp(s - jnp.max(s, axis=-1, keepdims=True))
        r = _dot(e.astype(BF16), mv) * (1.0 / jnp.sum(e, axis=-1, keepdims=True))
        mem_out = mem_out + jnp.where(hm, r, 0.0)

    nw = NA_WIDTH
    x1 = (x_ref[...] + _dot_tn(nat_ref[...], wout_ref[:nw])
          + _dot(hg_out.astype(BF16), wout_ref[nw:nw + w])
          + _dot(mem_out.astype(BF16), wout_ref[nw + w:]))
    x1_ref[...] = x1
    h2f = x1 * _rms_scale(x1) * fg_ref[...]
    h2t_ref[...] = h2f.T.astype(BF16)
    qp = _dot(h2f.astype(BF16), wq_ref[...]).astype(BF16)
    for hc in range(2 * PEER_HEADS):
        st_ref[hc] = _dot_nt(sk_ref[hc], qp[:, hc * PEER_NKEYS:(hc + 1) * PEER_NKEYS])


def _mix(na_t, o_f, o_b, hg, mq, kv, x2, hgn_g, wout_bf, ffn_g, wq_bf, sk_bf, *, seq, tm):
    n, d = x2.shape
    w = HG_WIDTH
    per_batch = seq // tm
    mem_tokens = kv.shape[0] // (n // seq)
    _, blockdiag = _hg_constants()
    blockones = jnp.asarray(blockdiag, BF16)
    nhc = 2 * PEER_HEADS

    def row(width, col=0):
        return pl.BlockSpec((tm, width), lambda i: (i, col))

    def full(a):
        return pl.BlockSpec(a.shape, lambda i: (0,) * a.ndim)

    return pl.pallas_call(
        _mix_kernel,
        grid=(n // tm,),
        in_specs=[pl.BlockSpec((NA_WIDTH, tm), lambda i: (0, i)), row(w), row(w), row(w, 4), row(MEM_WIDTH),
                  pl.BlockSpec((mem_tokens, kv.shape[1]), lambda i: (i // per_batch, 0)),
                  row(d), full(hgn_g), full(blockones), full(wout_bf), full(ffn_g), full(wq_bf), full(sk_bf)],
        out_specs=[row(d), pl.BlockSpec((d, tm), lambda i: (0, i)),
                   pl.BlockSpec((nhc, PEER_NKEYS, tm), lambda i: (0, 0, i))],
        out_shape=[jax.ShapeDtypeStruct((n, d), F32), jax.ShapeDtypeStruct((d, n), BF16),
                   jax.ShapeDtypeStruct((nhc, PEER_NKEYS, n), F32)],
        compiler_params=_cparams(("parallel",)),
        name="mix",
    )(na_t, o_f, o_b, hg, mq, kv, x2, hgn_g, blockones, wout_bf, ffn_g, wq_bf, sk_bf)


def _top_values(s, count, want_rank):
    cur = s
    vals = []
    for k in range(count):
        m = jnp.max(cur, axis=0, keepdims=True)
        vals.append(m)
        if k + 1 < count or want_rank:
            cur = jnp.where(cur >= m, -(k + 1) * KNOCK, cur)
    if not want_rank:
        return vals, None
    rank = jnp.where(cur <= -KNOCK, cur * (-1.0 / KNOCK) - 1.0, float(count))
    return vals, rank


def _route_kernel(st_ref, r2_ref, e2_ref, l1_ref, e1_ref):
    k = PEER_TOPK
    for h in range(PEER_HEADS):
        s1 = st_ref[2 * h]
        s2 = st_ref[2 * h + 1]
        v1, _ = _top_values(s1, k + 1, False)
        v2, r2 = _top_values(s2, k + 1, True)
        cand = [v1[a] + v2[b] for a in range(k + 1) for b in range((k + 1) // (a + 1))]
        pad = (-len(cand)) % 8
        cand = jnp.concatenate(cand + [jnp.full_like(cand[0], -jnp.inf)] * pad, axis=0)
        best, _ = _top_values(cand, k + 1, False)
        tau = 0.5 * (best[k - 1] + best[k])
        z = jnp.zeros_like(best[0])
        for c in best[:k]:
            z = z + jnp.exp(c - best[0])
        cnt = jnp.zeros(s1.shape, F32)
        for b in range(k):
            cnt = cnt + jnp.where(s1 >= tau - v2[b], 1.0, 0.0)
        r2_ref[h] = r2.astype(BF16)
        e2_ref[h] = jnp.exp(s2 - v2[0]).astype(BF16)
        l1_ref[h] = cnt
        e1_ref[h] = jnp.exp(s1 - v1[0]) * (1.0 / z)


def _route(st, *, tt):
    nhc, nk, n = st.shape
    heads = nhc // 2
    spec = pl.BlockSpec((heads, nk, tt), lambda i: (0, 0, i))
    return pl.pallas_call(
        _route_kernel,
        grid=(n // tt,),
        in_specs=[pl.BlockSpec((nhc, nk, tt), lambda i: (0, 0, i))],
        out_specs=[spec, spec, spec, spec],
        out_shape=[jax.ShapeDtypeStruct((heads, nk, n), BF16), jax.ShapeDtypeStruct((heads, nk, n), BF16),
                   jax.ShapeDtypeStruct((heads, nk, n), F32), jax.ShapeDtypeStruct((heads, nk, n), F32)],
        compiler_params=_cparams(("parallel",)),
        name="route",
    )(st)


def _gated_activations(act, r2_ref, e2_ref, l1_ref, e1_ref, c):
    nk = PEER_NKEYS
    tn = act.shape[1]
    pk = BF16_ROWS
    tiles = (nk // pk, pk, tn)

    def row_tile(ref, h, a, scale):
        return jnp.broadcast_to(scale * ref[h, c, a:a + 1, :], (pk, tn)).astype(BF16)[None]

    out = []
    for a in range(act.shape[0] // nk):
        half_gate = None
        for h in range(PEER_HEADS):
            limit = row_tile(l1_ref, h, a, 1.0)
            weight = row_tile(e1_ref, h, a, 0.5)
            term = jnp.where(r2_ref[h].reshape(tiles) < limit, e2_ref[h].reshape(tiles),
                             jnp.zeros((), BF16)) * weight
            half_gate = term if half_gate is None else half_gate + term
        x = act[a * nk:(a + 1) * nk].astype(BF16)
        out.append(x * (1.0 + lax.erf(x * (2.0 ** -0.5))) * half_gate.reshape(nk, tn))
    return jnp.concatenate(out, axis=0)


def _peer_kernel(h2t_ref, u_ref, vt_ref, r2_ref, e2_ref, l1_ref, e1_ref, x1_ref, fg_ref, o_ref, acc, *, ec):
    j = pl.program_id(1)

    @pl.when(j == 0)
    def _():
        acc[...] = jnp.zeros_like(acc)

    total = None
    for c in range(u_ref.shape[0] // ec):
        act = _dot(u_ref[c * ec:(c + 1) * ec], h2t_ref[...])
        w = _gated_activations(act, r2_ref, e2_ref, l1_ref, e1_ref, c)
        part = _dot(vt_ref[:, c * ec:(c + 1) * ec], w)
        total = part if total is None else total + part
    acc[...] += total

    @pl.when(j == pl.num_programs(1) - 1)
    def _():
        y = x1_ref[...] + acc[...].T
        o_ref[...] = y * _rms_scale(y) * fg_ref[...]


def _peer(h2t, u_bf, vt_bf, r2, e2, l1, e1, x1, final_g, *, tn, ec, per_step):
    n, d = x1.shape
    n_exp = u_bf.shape[0]
    groups = ec // PEER_NKEYS
    chunks = n_exp // ec
    heads, nk = PEER_HEADS, PEER_NKEYS
    l1 = l1.reshape(heads, chunks, groups, n)
    e1 = e1.reshape(heads, chunks, groups, n)
    tab = pl.BlockSpec((heads, nk, tn), lambda i, j: (0, 0, i))
    sel = pl.BlockSpec((heads, per_step, groups, tn), lambda i, j: (0, j, 0, i))
    return pl.pallas_call(
        functools.partial(_peer_kernel, ec=ec),
        grid=(n // tn, chunks // per_step),
        in_specs=[pl.BlockSpec((d, tn), lambda i, j: (0, i)),
                  pl.BlockSpec((per_step * ec, d), lambda i, j: (j, 0)),
                  pl.BlockSpec((d, per_step * ec), lambda i, j: (0, j)),
                  tab, tab, sel, sel,
                  pl.BlockSpec((tn, d), lambda i, j: (i, 0)),
                  pl.BlockSpec((1, d), lambda i, j: (0, 0))],
        out_specs=pl.BlockSpec((tn, d), lambda i, j: (i, 0)),
        out_shape=jax.ShapeDtypeStruct((n, d), F32),
        scratch_shapes=[pltpu.VMEM((d, tn), F32)],
        compiler_params=_cparams(("parallel", "arbitrary")),
        name="peer",
    )(h2t, u_bf, vt_bf, r2, e2, l1, e1, x1, final_g)


def kernel(x, mem, attn_norm_g, w_in, na_rpb, hg_lower_bound, hg_out_norm_g, mem_norm_g, w_mem_kv, w_out,
           ffn_norm_g, peer_w_query, peer_sub_keys, peer_u, peer_v, final_norm_g):
    batch, seq, d = x.shape
    n = batch * seq
    x2 = x.reshape(n, d)
    lb_table = jnp.cumsum(jax.nn.softmax(hg_lower_bound.astype(F32), axis=1), axis=1)
    qk, vt, hg, mq = _in_proj(x2, attn_norm_g[0].reshape(1, d), w_in[0].astype(BF16), tm=512)
    na_tab, na_mask = _na_tables(na_rpb[0])
    na_t = _na(qk, vt, na_tab, na_mask, batch=batch, n_rows=seq // GRID_W)
    o_f, o_b = _hgrn2(hg, lb_table[:, 0], batch=batch, seq=seq, chunks=8)
    kv = _mem_kv(mem.reshape(-1, d), mem_norm_g[0].reshape(1, d), w_mem_kv[0].astype(BF16), batch=batch)
    sk = peer_sub_keys[0].reshape(2 * PEER_HEADS, PEER_NKEYS, -1).astype(BF16)
    x1, h2t, st = _mix(na_t, o_f, o_b, hg, mq, kv, x2, hg_out_norm_g[0].reshape(1, -1), w_out[0].astype(BF16),
                      ffn_norm_g[0].reshape(1, d), peer_w_query[0].astype(BF16), sk, seq=seq, tm=512)
    r2, e2, l1, e1 = _route(st, tt=256)
    out = _peer(h2t, peer_u[0].astype(BF16), peer_v[0].T.astype(BF16), r2, e2, l1, e1, x1,
                final_norm_g.reshape(1, d), tn=512, ec=1024, per_step=2)
    return out.reshape(batch, seq, d)
```

```python
import functools
import math

import numpy as np
import jax
import jax.numpy as jnp
from jax import lax
from jax.experimental import pallas as pl
from jax.experimental.pallas import tpu as pltpu

F32 = jnp.float32
BF16 = jnp.bfloat16

RMS_EPS = 1e-6
GRID_W = 64
NA_HEADS = 8
NA_HEAD_DIM = 64
NA_WIDTH = NA_HEADS * NA_HEAD_DIM
NA_WIN_ROWS = 8
NA_WIN_COLS = 16
NA_GROUP_ROWS = 8
NA_QUAD_ROWS = 4
NA_KEY_ROWS = NA_QUAD_ROWS + NA_WIN_ROWS
NA_TABLE_ROWS = 2 * NA_WIN_ROWS - 1 + NA_QUAD_ROWS
HG_HEADS = 4
HG_DIM = 64
HG_WIDTH = HG_HEADS * HG_DIM
HG_CHUNK = 64
HG_SUB = 8
MEM_HEADS = 4
MEM_WIDTH = 256
PEER_HEADS = 8
PEER_NKEYS = 128
PEER_TOPK = 16
LANE = 128
BF16_ROWS = 16
NEG_BIG = -1e30
KNOCK = 2.0 ** 100

VMEM_LIMIT = 48 * 1024 * 1024


def _cparams(sem):
    return pltpu.CompilerParams(dimension_semantics=sem, vmem_limit_bytes=VMEM_LIMIT)


def _rms_scale(x):
    return lax.rsqrt(jnp.mean(x * x, axis=-1, keepdims=True) + RMS_EPS)


def _dot(a, b):
    return jnp.dot(a, b, preferred_element_type=F32)


def _dot_nt(a, b):
    return lax.dot_general(a, b, (((1,), (1,)), ((), ())), preferred_element_type=F32)


def _dot_tn(a, b):
    return lax.dot_general(a, b, (((0,), (0,)), ((), ())), preferred_element_type=F32)


def _in_proj_kernel(x_ref, g_ref, w_ref, wvt_ref, qk_ref, vt_ref, hg_ref, mq_ref):
    x = x_ref[...]
    h = (x * _rms_scale(x) * g_ref[...]).astype(BF16)
    nqk = qk_ref.shape[1]
    nv = vt_ref.shape[0]
    nh = hg_ref.shape[1]
    qk_ref[...] = _dot(h, w_ref[:, :nqk]).astype(BF16)
    vt_ref[...] = _dot_nt(wvt_ref[...], h).astype(BF16)
    hg_ref[...] = _dot(h, w_ref[:, nqk + nv:nqk + nv + nh])
    mq_ref[...] = _dot(h, w_ref[:, nqk + nv + nh:]).astype(BF16)


def _in_proj(x2, g, w_bf, *, tm):
    n, d = x2.shape
    nqk, nv, nh, nm = 2 * NA_WIDTH, NA_WIDTH, 5 * HG_WIDTH, MEM_WIDTH
    wvt = w_bf[:, nqk:nqk + nv].T
    return pl.pallas_call(
        _in_proj_kernel,
        grid=(n // tm,),
        in_specs=[
            pl.BlockSpec((tm, d), lambda i: (i, 0)),
            pl.BlockSpec((1, d), lambda i: (0, 0)),
            pl.BlockSpec(w_bf.shape, lambda i: (0, 0)),
            pl.BlockSpec((nv, d), lambda i: (0, 0)),
        ],
        out_specs=[
            pl.BlockSpec((tm, nqk), lambda i: (i, 0)),
            pl.BlockSpec((nv, tm), lambda i: (0, i)),
            pl.BlockSpec((tm, nh), lambda i: (i, 0)),
            pl.BlockSpec((tm, nm), lambda i: (i, 0)),
        ],
        out_shape=[
            jax.ShapeDtypeStruct((n, nqk), BF16),
            jax.ShapeDtypeStruct((nv, n), BF16),
            jax.ShapeDtypeStruct((n, nh), F32),
            jax.ShapeDtypeStruct((n, nm), BF16),
        ],
        compiler_params=_cparams(("parallel",)),
        name="in_proj",
    )(x2, g, w_bf, wvt)


def _na_tables(rpb):
    cols = np.arange(GRID_W)
    c0 = np.clip(cols - NA_WIN_COLS // 2, 0, GRID_W - NA_WIN_COLS)
    kc = np.arange(GRID_W)
    valid = (kc[None, :] >= c0[:, None]) & (kc[None, :] < c0[:, None] + NA_WIN_COLS)
    dc = kc[None, :] - cols[:, None] + (NA_WIN_COLS - 1)
    onehot = (dc[None] == np.arange(2 * NA_WIN_COLS - 1)[:, None, None]) & valid[None]
    band = jnp.einsum('hrd,dqk->hrkq', rpb.astype(F32), jnp.asarray(onehot, F32),
                      precision=lax.Precision.HIGHEST)
    band = band + jnp.asarray(np.where(valid.T, 0.0, NEG_BIG), F32)
    heads, n_dr = band.shape[:2]
    quad, n_e = NA_QUAD_ROWS, NA_TABLE_ROWS
    neg = lambda k: jnp.full((heads, k, GRID_W, GRID_W), NEG_BIG, F32)
    padded = jnp.concatenate([neg(quad - 1), band, neg(n_e - n_dr)], axis=1)
    tab = jnp.concatenate([padded[:, quad - 1 - t:quad - 1 - t + n_e] for t in range(quad)], axis=-1)
    tab = tab.reshape(heads, n_e * GRID_W, quad * GRID_W)
    i = np.arange(NA_KEY_ROWS)[:, None]
    t = np.arange(quad)[None, :]
    inside = np.stack([(i >= t) & (i < t + NA_WIN_ROWS), (i < NA_WIN_ROWS) & (t >= 0)])
    mask = np.where(inside, 0.0, NEG_BIG).astype(np.float32)
    mask = np.broadcast_to(mask[:, :, None, :, None], (2, NA_KEY_ROWS, GRID_W, quad, GRID_W))
    return tab, jnp.asarray(mask.reshape(2, NA_KEY_ROWS * GRID_W, quad * GRID_W))


def _na_kernel(q_ref, kp_ref, kc_ref, kn_ref, vp_ref, vc_ref, vn_ref, tab_ref, mask_ref, o_ref, k_scr, v_scr,
               s_scr, p_scr, inv_scr, *, n_rows):
    g = pl.program_id(1)
    gt = NA_GROUP_ROWS * GRID_W
    for blk, ref in enumerate((kp_ref, kc_ref, kn_ref)):
        k_scr[blk * gt:(blk + 1) * gt] = ref[...]
    pairs_per_block = gt // LANE
    for blk, ref in enumerate((vp_ref, vc_ref, vn_ref)):
        for t in range(pairs_per_block):
            v_scr[blk * pairs_per_block + t] = ref[:, t * LANE:(t + 1) * LANE]
    first_head_lanes = lax.broadcasted_iota(jnp.int32, (1, LANE), 1) < NA_HEAD_DIM
    first_head_rows = lax.broadcasted_iota(jnp.int32, (LANE, 1), 0) < NA_HEAD_DIM
    scale = NA_HEAD_DIM ** -0.5
    qt = NA_QUAD_ROWS * GRID_W
    nkeys = NA_KEY_ROWS * GRID_W

    def params(quad):
        r = g * NA_GROUP_ROWS + quad * NA_QUAD_ROWS
        w0 = jnp.clip(r - NA_WIN_ROWS // 2, 0, n_rows - NA_WIN_ROWS)
        rel = w0 - g * NA_GROUP_ROWS + NA_GROUP_ROWS
        koff = pl.multiple_of(rel * GRID_W, LANE)
        pr0 = lax.shift_right_logical(rel, 1)
        toff = pl.multiple_of((w0 - r + NA_WIN_ROWS - 1) * GRID_W, GRID_W)
        edge = jnp.logical_or(r == 0, r == n_rows - NA_QUAD_ROWS)
        return koff, pr0, toff, jnp.where(edge, 1, 0)

    def scores(quad):
        koff, _, toff, cls = params(quad)
        mask = mask_ref[cls]
        for p in range(NA_HEADS // 2):
            cs = slice(p * LANE, (p + 1) * LANE)
            qp = q_ref[quad * qt:(quad + 1) * qt, cs] * scale
            zero = jnp.zeros_like(qp)
            kw = k_scr[pl.ds(koff, nkeys), cs]
            for hh in range(2):
                qm = (jnp.where(first_head_lanes, qp, zero) if hh == 0
                      else jnp.where(first_head_lanes, zero, qp))
                s_scr[quad, 2 * p + hh] = _dot_nt(kw, qm) + tab_ref[2 * p + hh, pl.ds(toff, nkeys), :] + mask

    def softmax(quad):
        for h in range(NA_HEADS):
            s = s_scr[quad, h]
            e = jnp.exp(s - jnp.max(s, axis=0, keepdims=True))
            inv = 1.0 / jnp.sum(e, axis=0, keepdims=True)
            inv_scr[quad, h] = jnp.broadcast_to(inv, inv_scr.shape[2:])
            p_scr[quad, h] = e.astype(BF16)

    def values(quad):
        _, pr0, _, _ = params(quad)
        for p in range(NA_HEADS // 2):
            cs = slice(p * LANE, (p + 1) * LANE)
            vwin = jnp.concatenate([v_scr[pr0 + i, cs, :] for i in range(nkeys // LANE)], axis=1)
            outs = [_dot(vwin, p_scr[quad, 2 * p + hh]) * inv_scr[quad, 2 * p + hh, 0:1] for hh in range(2)]
            o_ref[cs, quad * qt:(quad + 1) * qt] = jnp.where(
                first_head_rows, outs[0], outs[1]).astype(o_ref.dtype)

    @pl.when(g >= 0)
    def _():
        scores(0)

    @pl.when(g >= 0)
    def _():
        softmax(0)
        scores(1)

    @pl.when(g >= 0)
    def _():
        values(0)
        softmax(1)

    @pl.when(g >= 0)
    def _():
        values(1)


def _na(qk, vt, tab, mask, *, batch, n_rows):
    n = qk.shape[0]
    gt = NA_GROUP_ROWS * GRID_W
    ng = n_rows // NA_GROUP_ROWS
    w = NA_WIDTH
    nkeys, nq = NA_KEY_ROWS * GRID_W, NA_QUAD_ROWS * GRID_W
    quads = NA_GROUP_ROWS // NA_QUAD_ROWS
    assert quads == 2, "the kernel body pipelines exactly two quads per group"

    def blk(g, shift):
        return jnp.clip(g + shift, 0, ng - 1)

    def kspec(shift):
        return pl.BlockSpec((gt, w), lambda b, g: (b * ng + blk(g, shift), 1))

    def vspec(shift):
        return pl.BlockSpec((w, gt), lambda b, g: (0, b * ng + blk(g, shift)))

    return pl.pallas_call(
        functools.partial(_na_kernel, n_rows=n_rows),
        grid=(batch, ng),
        in_specs=[pl.BlockSpec((gt, w), lambda b, g: (b * ng + g, 0)),
                  kspec(-1), kspec(0), kspec(1), vspec(-1), vspec(0), vspec(1),
                  pl.BlockSpec(tab.shape, lambda b, g: (0, 0, 0), pipeline_mode=pl.Buffered(1)),
                  pl.BlockSpec(mask.shape, lambda b, g: (0, 0, 0))],
        out_specs=pl.BlockSpec((w, gt), lambda b, g: (0, b * ng + g)),
        out_shape=jax.ShapeDtypeStruct((w, n), BF16),
        scratch_shapes=[pltpu.VMEM((3 * gt, w), BF16), pltpu.VMEM((3 * gt // LANE, w, LANE), BF16),
                        pltpu.VMEM((quads, NA_HEADS, nkeys, nq), F32),
                        pltpu.VMEM((quads, NA_HEADS, nkeys, nq), BF16),
                        pltpu.VMEM((quads, NA_HEADS, 8, nq), F32)],
        compiler_params=_cparams(("parallel", "parallel")),
        name="na",
    )(qk, qk, qk, qk, vt, vt, vt, tab, mask)


def _hg_constants():
    c, sub = HG_CHUNK, HG_SUB
    t = np.arange(c)[:, None]
    s = np.arange(c)[None, :]
    i8 = np.arange(sub)[:, None]
    mats = []
    for rev in (False, True):
        if not rev:
            inc = s <= t
            rb = s < sub * (t // sub)
            eb = s <= sub * (t // sub) + sub - 1
            r8 = s < sub * i8
        else:
            inc = s >= t
            rb = s > sub * (t // sub) + sub - 1
            eb = s >= sub * (t // sub)
            r8 = s > sub * i8 + sub - 1
        extra = np.zeros((16, c), bool)
        extra[:8] = r8
        extra[8] = True
        mats.append(np.concatenate([inc, rb, eb, extra], axis=0))
    tmat = np.stack(mats).astype(np.float32)
    lane_head = np.arange(HG_WIDTH) // HG_DIM
    blockdiag = (lane_head[:, None] == lane_head[None, :]).astype(np.float32)
    return tmat, blockdiag


def _hg_chunk(q, fpre, v, lb, st, tmat, blockones, rev):
    c, sub, nsub, w = HG_CHUNK, HG_SUB, HG_CHUNK // HG_SUB, HG_WIDTH
    f = lb + (1.0 - lb) * jax.nn.sigmoid(fpre)
    kk = 1.0 - f
    lf = jnp.log(f)
    lf_hi = lf.astype(BF16)
    lf_lo = (lf - lf_hi.astype(F32)).astype(BF16)
    cs = _dot(tmat, lf_hi) + _dot(tmat, lf_lo)
    b, rb, eb = cs[0:c], cs[c:2 * c], cs[2 * c:3 * c]
    r8 = cs[3 * c:3 * c + nsub]
    tot = cs[3 * c + nsub:3 * c + nsub + 1]

    lane = lax.broadcasted_iota(jnp.int32, (1, w), 1)
    head_masks = [(lane // HG_DIM) == h for h in range(HG_HEADS)]
    col_sub = (lane % HG_DIM) // sub

    qt = q * jnp.exp(b - rb)
    kt = kk * jnp.exp(eb - b)

    blocks, pairs = [], []
    for i in range(nsub):
        js = range(i) if not rev else range(i + 1, nsub)
        if len(js) == 0:
            continue
        mid = jnp.exp(r8[i:i + 1] - r8)
        for j in js:
            m = j + 1 if not rev else j - 1
            blocks.append(qt[sub * i:sub * (i + 1)] * mid[m:m + 1])
            pairs.append((i, j))
    qbig = jnp.concatenate(blocks, axis=0).astype(BF16)
    zero = jnp.zeros_like(kt)
    kstack = jnp.concatenate([jnp.where(hm, kt, zero) for hm in head_masks], axis=0).astype(BF16)
    roff = _dot_nt(qbig, kstack)
    rows = [jnp.zeros((sub, w), F32)] * nsub
    for n, (i, j) in enumerate(pairs):
        rows[i] = rows[i] + jnp.where(col_sub == j, roff[sub * n:sub * (n + 1)], 0.0)
    a_off = jnp.concatenate(rows, axis=0).astype(BF16)
    vstack = jnp.concatenate([jnp.where(hm, v, zero) for hm in head_masks], axis=0).astype(BF16)
    o = _dot(a_off, vstack)

    row_in_sub = lax.broadcasted_iota(jnp.int32, (nsub, sub, w), 1)
    b3, kk3, v3 = b.reshape(nsub, sub, w), kk.reshape(nsub, sub, w), v.reshape(nsub, sub, w)
    q3 = q.reshape(nsub, sub, w)
    dparts, vparts = [q3 * kk3], [v3]
    for d in range(1, sub):
        shift = d if not rev else sub - d
        valid = (row_in_sub >= d) if not rev else (row_in_sub + d < sub)
        bd = pltpu.roll(b3, shift, axis=1)
        kd = pltpu.roll(kk3, shift, axis=1)
        vparts.append(pltpu.roll(v3, shift, axis=1))
        ex = jnp.where(valid, b3 - bd, 0.0)
        dparts.append(jnp.where(valid, q3 * kd * jnp.exp(ex), 0.0))
    dstack = jnp.concatenate(dparts, axis=0).reshape(sub * c, w).astype(BF16)
    abc = _dot(dstack, blockones).reshape(sub, nsub, sub, w)
    for d in range(sub):
        o = o + (abc[d] * vparts[d]).reshape(c, w)

    o = o + _dot_nt((q * jnp.exp(b)).astype(BF16), st.astype(BF16))
    kdec = (kk * jnp.exp(tot - b)).astype(BF16)
    ut = _dot_tn(v.astype(BF16), kdec)
    st_new = st * jnp.exp(tot) + ut * blockones.astype(F32)
    return o, st_new


def _hg_kernel(qf_ref, ff_ref, vf_ref, qb_ref, fb_ref, vb_ref, lb_ref, tmat_ref, ones_ref,
               of_ref, ob_ref, sf_ref, sb_ref, *, chunks):
    @pl.when(pl.program_id(1) == 0)
    def _():
        sf_ref[...] = jnp.zeros_like(sf_ref)
        sb_ref[...] = jnp.zeros_like(sb_ref)

    blockones = ones_ref[...]
    lbf, lbb = lb_ref[0:1], lb_ref[1:2]

    def body(ci, carry):
        rf = pl.ds(pl.multiple_of(ci * HG_CHUNK, HG_CHUNK), HG_CHUNK)
        rb = pl.ds(pl.multiple_of((chunks - 1 - ci) * HG_CHUNK, HG_CHUNK), HG_CHUNK)
        o_f, s_f = _hg_chunk(qf_ref[rf], ff_ref[rf], vf_ref[rf], lbf, sf_ref[...], tmat_ref[0], blockones, False)
        of_ref[rf] = o_f
        sf_ref[...] = s_f
        o_b, s_b = _hg_chunk(qb_ref[rb], fb_ref[rb], vb_ref[rb], lbb, sb_ref[...], tmat_ref[1], blockones, True)
        ob_ref[rb] = o_b
        sb_ref[...] = s_b
        return carry

    lax.fori_loop(0, chunks, body, 0)


def _hgrn2(hg, lb2, *, batch, seq, chunks):
    n = hg.shape[0]
    tb = chunks * HG_CHUNK
    nb = seq // tb
    w = HG_WIDTH
    tmat, blockdiag = _hg_constants()
    tmat = jnp.asarray(tmat, BF16)
    blockones = jnp.asarray(blockdiag, BF16)

    def fwd(col):
        return pl.BlockSpec((tb, w), lambda b, i: (b * nb + i, col))

    def bwd(col):
        return pl.BlockSpec((tb, w), lambda b, i: (b * nb + nb - 1 - i, col))

    return pl.pallas_call(
        functools.partial(_hg_kernel, chunks=chunks),
        grid=(batch, nb),
        in_specs=[fwd(0), fwd(1), fwd(3), bwd(0), bwd(2), bwd(3),
                  pl.BlockSpec((2, w), lambda b, i: (0, 0)),
                  pl.BlockSpec(tmat.shape, lambda b, i: (0, 0, 0)),
                  pl.BlockSpec((w, w), lambda b, i: (0, 0))],
        out_specs=[fwd(0), bwd(0)],
        out_shape=[jax.ShapeDtypeStruct((n, w), F32), jax.ShapeDtypeStruct((n, w), F32)],
        scratch_shapes=[pltpu.VMEM((w, w), F32), pltpu.VMEM((w, w), F32)],
        compiler_params=_cparams(("parallel", "arbitrary")),
        name="hgrn2",
    )(hg, hg, hg, hg, hg, hg, lb2, tmat, blockones)


def _mem_kv_kernel(m_ref, g_ref, w_ref, o_ref):
    m = m_ref[...]
    h = (m * _rms_scale(m) * g_ref[...]).astype(BF16)
    o_ref[...] = _dot(h, w_ref[...]).astype(BF16)


def _mem_kv(mem2, g, w_bf, *, batch):
    n, d = mem2.shape
    m = n // batch
    return pl.pallas_call(
        _mem_kv_kernel,
        grid=(batch,),
        in_specs=[pl.BlockSpec((m, d), lambda b: (b, 0)),
                  pl.BlockSpec((1, d), lambda b: (0, 0)),
                  pl.BlockSpec(w_bf.shape, lambda b: (0, 0))],
        out_specs=pl.BlockSpec((m, w_bf.shape[1]), lambda b: (b, 0)),
        out_shape=jax.ShapeDtypeStruct((n, w_bf.shape[1]), BF16),
        compiler_params=_cparams(("parallel",)),
        name="mem_kv",
    )(mem2, g, w_bf)


def _mix_kernel(nat_ref, of_ref, ob_ref, gate_ref, mq_ref, kv_ref, x_ref, hgn_ref, ones_ref, wout_ref,
                fg_ref, wq_ref, sk_ref, x1_ref, h2t_ref, st_ref):
    w = HG_WIDTH
    lane = lax.broadcasted_iota(jnp.int32, (1, w), 1)
    head_masks = [(lane // HG_DIM) == h for h in range(HG_HEADS)]
    ones = ones_ref[...]

    o = of_ref[...] + ob_ref[...]
    o2 = o * o
    o2_hi = o2.astype(BF16)
    o2_lo = (o2 - o2_hi.astype(F32)).astype(BF16)
    ms = (_dot(o2_hi, ones) + _dot(o2_lo, ones)) * (1.0 / HG_DIM)
    g = gate_ref[...]
    hg_out = o * lax.rsqrt(ms + RMS_EPS) * hgn_ref[...] * (g * jax.nn.sigmoid(g))

    mq = mq_ref[...] * (MEM_WIDTH // MEM_HEADS) ** -0.5
    mk = kv_ref[:, :MEM_WIDTH]
    mv = kv_ref[:, MEM_WIDTH:]
    zero = jnp.zeros_like(mq)
    mem_out = jnp.zeros(mq.shape, F32)
    for hm in head_masks:
        s = _dot_nt(jnp.where(hm, mq, zero), mk)
        e = jnp.exp(s - jnp.max(s, axis=-1, keepdims=True))
        r = _dot(e.astype(BF16), mv) * (1.0 / jnp.sum(e, axis=-1, keepdims=True))
        mem_out = mem_out + jnp.where(hm, r, 0.0)

    nw = NA_WIDTH
    x1 = (x_ref[...] + _dot_tn(nat_ref[...], wout_ref[:nw])
          + _dot(hg_out.astype(BF16), wout_ref[nw:nw + w])
          + _dot(mem_out.astype(BF16), wout_ref[nw + w:]))
    x1_ref[...] = x1
    h2f = x1 * _rms_scale(x1) * fg_ref[...]
    h2t_ref[...] = h2f.T.astype(BF16)
    qp = _dot(h2f.astype(BF16), wq_ref[...]).astype(BF16)
    for hc in range(2 * PEER_HEADS):
        st_ref[hc] = _dot_nt(sk_ref[hc], qp[:, hc * PEER_NKEYS:(hc + 1) * PEER_NKEYS])


def _mix(na_t, o_f, o_b, hg, mq, kv, x2, hgn_g, wout_bf, ffn_g, wq_bf, sk_bf, *, seq, tm):
    n, d = x2.shape
    w = HG_WIDTH
    per_batch = seq // tm
    mem_tokens = kv.shape[0] // (n // seq)
    _, blockdiag = _hg_constants()
    blockones = jnp.asarray(blockdiag, BF16)
    nhc = 2 * PEER_HEADS

    def row(width, col=0):
        return pl.BlockSpec((tm, width), lambda i: (i, col))

    def full(a):
        return pl.BlockSpec(a.shape, lambda i: (0,) * a.ndim)

    return pl.pallas_call(
        _mix_kernel,
        grid=(n // tm,),
        in_specs=[pl.BlockSpec((NA_WIDTH, tm), lambda i: (0, i)), row(w), row(w), row(w, 4), row(MEM_WIDTH),
                  pl.BlockSpec((mem_tokens, kv.shape[1]), lambda i: (i // per_batch, 0)),
                  row(d), full(hgn_g), full(blockones), full(wout_bf), full(ffn_g), full(wq_bf), full(sk_bf)],
        out_specs=[row(d), pl.BlockSpec((d, tm), lambda i: (0, i)),
                   pl.BlockSpec((nhc, PEER_NKEYS, tm), lambda i: (0, 0, i))],
        out_shape=[jax.ShapeDtypeStruct((n, d), F32), jax.ShapeDtypeStruct((d, n), BF16),
                   jax.ShapeDtypeStruct((nhc, PEER_NKEYS, n), F32)],
        compiler_params=_cparams(("parallel",)),
        name="mix",
    )(na_t, o_f, o_b, hg, mq, kv, x2, hgn_g, blockones, wout_bf, ffn_g, wq_bf, sk_bf)


def _top_values(s, count, want_rank):
    cur = s
    vals = []
    for k in range(count):
        m = jnp.max(cur, axis=0, keepdims=True)
        vals.append(m)
        if k + 1 < count or want_rank:
            cur = jnp.where(cur >= m, -(k + 1) * KNOCK, cur)
    if not want_rank:
        return vals, None
    rank = jnp.where(cur <= -KNOCK, cur * (-1.0 / KNOCK) - 1.0, float(count))
    return vals, rank


def _route_kernel(st_ref, r2_ref, e2_ref, l1_ref, e1_ref):
    k = PEER_TOPK
    for h in range(PEER_HEADS):
        s1 = st_ref[2 * h]
        s2 = st_ref[2 * h + 1]
        v1, _ = _top_values(s1, k + 1, False)
        v2, r2 = _top_values(s2, k + 1, True)
        cand = [v1[a] + v2[b] for a in range(k + 1) for b in range((k + 1) // (a + 1))]
        pad = (-len(cand)) % 8
        cand = jnp.concatenate(cand + [jnp.full_like(cand[0], -jnp.inf)] * pad, axis=0)
        best, _ = _top_values(cand, k + 1, False)
        tau = 0.5 * (best[k - 1] + best[k])
        z = jnp.zeros_like(best[0])
        for c in best[:k]:
            z = z + jnp.exp(c - best[0])
        cnt = jnp.zeros(s1.shape, F32)
        for b in range(k):
            cnt = cnt + jnp.where(s1 >= tau - v2[b], 1.0, 0.0)
        r2_ref[h] = r2.astype(BF16)
        e2_ref[h] = jnp.exp(s2 - v2[0]).astype(BF16)
        l1_ref[h] = cnt
        e1_ref[h] = jnp.exp(s1 - v1[0]) * (1.0 / z)


def _route(st, *, tt):
    nhc, nk, n = st.shape
    heads = nhc // 2
    spec = pl.BlockSpec((heads, nk, tt), lambda i: (0, 0, i))
    return pl.pallas_call(
        _route_kernel,
        grid=(n // tt,),
        in_specs=[pl.BlockSpec((nhc, nk, tt), lambda i: (0, 0, i))],
        out_specs=[spec, spec, spec, spec],
        out_shape=[jax.ShapeDtypeStruct((heads, nk, n), BF16), jax.ShapeDtypeStruct((heads, nk, n), BF16),
                   jax.ShapeDtypeStruct((heads, nk, n), F32), jax.ShapeDtypeStruct((heads, nk, n), F32)],
        compiler_params=_cparams(("parallel",)),
        name="route",
    )(st)


def _gated_activations(act, r2_ref, e2_ref, l1_ref, e1_ref, c):
    nk = PEER_NKEYS
    tn = act.shape[1]
    pk = BF16_ROWS
    tiles = (nk // pk, pk, tn)

    def row_tile(ref, h, a, scale):
        return jnp.broadcast_to(scale * ref[h, c, a:a + 1, :], (pk, tn)).astype(BF16)[None]

    out = []
    for a in range(act.shape[0] // nk):
        half_gate = None
        for h in range(PEER_HEADS):
            limit = row_tile(l1_ref, h, a, 1.0)
            weight = row_tile(e1_ref, h, a, 0.5)
            term = jnp.where(r2_ref[h].reshape(tiles) < limit, e2_ref[h].reshape(tiles),
                             jnp.zeros((), BF16)) * weight
            half_gate = term if half_gate is None else half_gate + term
        x = act[a * nk:(a + 1) * nk].astype(BF16)
        out.append(x * (1.0 + lax.erf(x * (2.0 ** -0.5))) * half_gate.reshape(nk, tn))
    return jnp.concatenate(out, axis=0)


def _peer_kernel(h2t_ref, u_ref, vt_ref, r2_ref, e2_ref, l1_ref, e1_ref, x1_ref, fg_ref, o_ref, *scratch, ec):
    j = pl.program_id(1)
    acc = scratch[-1]
    nch = (len(scratch) - 1) // 2
    a_bufs, w_bufs = scratch[:nch], scratch[nch:2 * nch]

    @pl.when(j == 0)
    def _():
        acc[...] = jnp.zeros_like(acc)

    def stage_a(c):
        a_bufs[c][...] = _dot(u_ref[c * ec:(c + 1) * ec], h2t_ref[...])

    def stage_b(c):
        w_bufs[c][...] = _gated_activations(a_bufs[c][...], r2_ref, e2_ref, l1_ref, e1_ref, c)

    def stage_c(c):
        acc[...] += _dot(vt_ref[:, c * ec:(c + 1) * ec], w_bufs[c][...])

    for t in range(nch + 2):
        @pl.when(j >= 0)
        def _(t=t):
            if 0 <= t - 2 < nch:
                stage_c(t - 2)
            if 0 <= t - 1 < nch:
                stage_b(t - 1)
            if t < nch:
                stage_a(t)

    @pl.when(j == pl.num_programs(1) - 1)
    def _():
        y = x1_ref[...] + acc[...].T
        o_ref[...] = y * _rms_scale(y) * fg_ref[...]


def _peer(h2t, u_bf, vt_bf, r2, e2, l1, e1, x1, final_g, *, tn, ec, per_step):
    n, d = x1.shape
    n_exp = u_bf.shape[0]
    groups = ec // PEER_NKEYS
    chunks = n_exp // ec
    heads, nk = PEER_HEADS, PEER_NKEYS
    l1 = l1.reshape(heads, chunks, groups, n)
    e1 = e1.reshape(heads, chunks, groups, n)
    tab = pl.BlockSpec((heads, nk, tn), lambda i, j: (0, 0, i))
    sel = pl.BlockSpec((heads, per_step, groups, tn), lambda i, j: (0, j, 0, i))
    return pl.pallas_call(
        functools.partial(_peer_kernel, ec=ec),
        grid=(n // tn, chunks // per_step),
        in_specs=[pl.BlockSpec((d, tn), lambda i, j: (0, i)),
                  pl.BlockSpec((per_step * ec, d), lambda i, j: (j, 0)),
                  pl.BlockSpec((d, per_step * ec), lambda i, j: (0, j)),
                  tab, tab, sel, sel,
                  pl.BlockSpec((tn, d), lambda i, j: (i, 0)),
                  pl.BlockSpec((1, d), lambda i, j: (0, 0))],
        out_specs=pl.BlockSpec((tn, d), lambda i, j: (i, 0)),
        out_shape=jax.ShapeDtypeStruct((n, d), F32),
        scratch_shapes=([pltpu.VMEM((ec, tn), F32)] * per_step + [pltpu.VMEM((ec, tn), BF16)] * per_step
                        + [pltpu.VMEM((d, tn), F32)]),
        compiler_params=_cparams(("parallel", "arbitrary")),
        name="peer",
    )(h2t, u_bf, vt_bf, r2, e2, l1, e1, x1, final_g)


def kernel(x, mem, attn_norm_g, w_in, na_rpb, hg_lower_bound, hg_out_norm_g, mem_norm_g, w_mem_kv, w_out,
           ffn_norm_g, peer_w_query, peer_sub_keys, peer_u, peer_v, final_norm_g):
    batch, seq, d = x.shape
    n = batch * seq
    x2 = x.reshape(n, d)
    lb_table = jnp.cumsum(jax.nn.softmax(hg_lower_bound.astype(F32), axis=1), axis=1)
    qk, vt, hg, mq = _in_proj(x2, attn_norm_g[0].reshape(1, d), w_in[0].astype(BF16), tm=512)
    na_tab, na_mask = _na_tables(na_rpb[0])
    na_t = _na(qk, vt, na_tab, na_mask, batch=batch, n_rows=seq // GRID_W)
    o_f, o_b = _hgrn2(hg, lb_table[:, 0], batch=batch, seq=seq, chunks=8)
    kv = _mem_kv(mem.reshape(-1, d), mem_norm_g[0].reshape(1, d), w_mem_kv[0].astype(BF16), batch=batch)
    sk = peer_sub_keys[0].reshape(2 * PEER_HEADS, PEER_NKEYS, -1).astype(BF16)
    x1, h2t, st = _mix(na_t, o_f, o_b, hg, mq, kv, x2, hg_out_norm_g[0].reshape(1, -1), w_out[0].astype(BF16),
                      ffn_norm_g[0].reshape(1, d), peer_w_query[0].astype(BF16), sk, seq=seq, tm=512)
    r2, e2, l1, e1 = _route(st, tt=256)
    out = _peer(h2t, peer_u[0].astype(BF16), peer_v[0].T.astype(BF16), r2, e2, l1, e1, x1,
                final_norm_g.reshape(1, d), tn=512, ec=1024, per_step=2)
    return out.reshape(batch, seq, d)
```

```python
import functools
import math

import numpy as np
import jax
import jax.numpy as jnp
from jax import lax
from jax.experimental import pallas as pl
from jax.experimental.pallas import tpu as pltpu

F32 = jnp.float32
BF16 = jnp.bfloat16

RMS_EPS = 1e-6
GRID_W = 64
NA_HEADS = 8
NA_HEAD_DIM = 64
NA_WIDTH = NA_HEADS * NA_HEAD_DIM
NA_WIN_ROWS = 8
NA_WIN_COLS = 16
NA_GROUP_ROWS = 8
NA_QUAD_ROWS = 4
NA_KEY_ROWS = NA_QUAD_ROWS + NA_WIN_ROWS
NA_TABLE_ROWS = 2 * NA_WIN_ROWS - 1 + NA_QUAD_ROWS
HG_HEADS = 4
HG_DIM = 64
HG_WIDTH = HG_HEADS * HG_DIM
HG_CHUNK = 64
HG_SUB = 8
MEM_HEADS = 4
MEM_WIDTH = 256
PEER_HEADS = 8
PEER_NKEYS = 128
PEER_TOPK = 16
LANE = 128
F32_ROWS = 8
BF16_ROWS = 16
NEG_BIG = -1e30
KNOCK = 2.0 ** 100
NEAR_TIE_REL = 2.0 ** -20
NEAR_TIE_ABS = 1e-30

VMEM_LIMIT = 48 * 1024 * 1024

IN_PROJ_TOKENS = 512
HG_CHUNKS_PER_STEP = 8
MIX_TOKENS = 512
ROUTE_TOKENS = 128
PEER_TOKENS = 512
PEER_CHUNK = 1024
PEER_CHUNKS_PER_STEP = 2


def _cparams(sem):
    return pltpu.CompilerParams(dimension_semantics=sem, vmem_limit_bytes=VMEM_LIMIT)


def _rms_scale(x):
    return lax.rsqrt(jnp.mean(x * x, axis=-1, keepdims=True) + RMS_EPS)


def _dot(a, b):
    return jnp.dot(a, b, preferred_element_type=F32)


def _dot_nt(a, b):
    return lax.dot_general(a, b, (((1,), (1,)), ((), ())), preferred_element_type=F32)


def _dot_tn(a, b):
    return lax.dot_general(a, b, (((0,), (0,)), ((), ())), preferred_element_type=F32)


def _in_proj_kernel(x_ref, g_ref, w_ref, wvt_ref, qk_ref, vt_ref, hg_ref, mq_ref):
    x = x_ref[...]
    h = (x * _rms_scale(x) * g_ref[...]).astype(BF16)
    nqk = qk_ref.shape[1]
    nv = vt_ref.shape[0]
    nh = hg_ref.shape[1]
    qk_ref[...] = _dot(h, w_ref[:, :nqk]).astype(BF16)
    vt_ref[...] = _dot_nt(wvt_ref[...], h).astype(BF16)
    hg_ref[...] = _dot(h, w_ref[:, nqk + nv:nqk + nv + nh])
    mq_ref[...] = _dot(h, w_ref[:, nqk + nv + nh:]).astype(BF16)


def _in_proj(x2, g, w_bf, *, tm):
    n, d = x2.shape
    nqk, nv, nh, nm = 2 * NA_WIDTH, NA_WIDTH, 5 * HG_WIDTH, MEM_WIDTH
    wvt = w_bf[:, nqk:nqk + nv].T
    return pl.pallas_call(
        _in_proj_kernel,
        grid=(n // tm,),
        in_specs=[
            pl.BlockSpec((tm, d), lambda i: (i, 0)),
            pl.BlockSpec((1, d), lambda i: (0, 0)),
            pl.BlockSpec(w_bf.shape, lambda i: (0, 0)),
            pl.BlockSpec((nv, d), lambda i: (0, 0)),
        ],
        out_specs=[
            pl.BlockSpec((tm, nqk), lambda i: (i, 0)),
            pl.BlockSpec((nv, tm), lambda i: (0, i)),
            pl.BlockSpec((tm, nh), lambda i: (i, 0)),
            pl.BlockSpec((tm, nm), lambda i: (i, 0)),
        ],
        out_shape=[
            jax.ShapeDtypeStruct((n, nqk), BF16),
            jax.ShapeDtypeStruct((nv, n), BF16),
            jax.ShapeDtypeStruct((n, nh), F32),
            jax.ShapeDtypeStruct((n, nm), BF16),
        ],
        compiler_params=_cparams(("parallel",)),
        name="in_proj",
    )(x2, g, w_bf, wvt)


def _na_tables(rpb):
    cols = np.arange(GRID_W)
    c0 = np.clip(cols - NA_WIN_COLS // 2, 0, GRID_W - NA_WIN_COLS)
    kc = np.arange(GRID_W)
    valid = (kc[None, :] >= c0[:, None]) & (kc[None, :] < c0[:, None] + NA_WIN_COLS)
    dc = kc[None, :] - cols[:, None] + (NA_WIN_COLS - 1)
    onehot = (dc[None] == np.arange(2 * NA_WIN_COLS - 1)[:, None, None]) & valid[None]
    band = jnp.einsum('hrd,dqk->hrkq', rpb.astype(F32), jnp.asarray(onehot, F32),
                      precision=lax.Precision.HIGHEST)
    band = band + jnp.asarray(np.where(valid.T, 0.0, NEG_BIG), F32)
    heads, n_dr = band.shape[:2]
    quad, n_e = NA_QUAD_ROWS, NA_TABLE_ROWS
    neg = lambda k: jnp.full((heads, k, GRID_W, GRID_W), NEG_BIG, F32)
    padded = jnp.concatenate([neg(quad - 1), band, neg(n_e - n_dr)], axis=1)
    tab = jnp.concatenate([padded[:, quad - 1 - t:quad - 1 - t + n_e] for t in range(quad)], axis=-1)
    tab = tab.reshape(heads, n_e * GRID_W, quad * GRID_W)
    i = np.arange(NA_KEY_ROWS)[:, None]
    t = np.arange(quad)[None, :]
    inside = np.stack([(i >= t) & (i < t + NA_WIN_ROWS), (i < NA_WIN_ROWS) & (t >= 0)])
    mask = np.where(inside, 0.0, NEG_BIG).astype(np.float32)
    mask = np.broadcast_to(mask[:, :, None, :, None], (2, NA_KEY_ROWS, GRID_W, quad, GRID_W))
    return tab, jnp.asarray(mask.reshape(2, NA_KEY_ROWS * GRID_W, quad * GRID_W))


def _na_kernel(q_ref, kp_ref, kc_ref, kn_ref, vp_ref, vc_ref, vn_ref, tab_ref, mask_ref, o_ref, k_scr, v_scr,
               s_scr, p_scr, inv_scr, *, n_rows):
    g = pl.program_id(1)
    gt = NA_GROUP_ROWS * GRID_W
    for blk, ref in enumerate((kp_ref, kc_ref, kn_ref)):
        k_scr[blk * gt:(blk + 1) * gt] = ref[...]
    pairs_per_block = gt // LANE
    for blk, ref in enumerate((vp_ref, vc_ref, vn_ref)):
        for t in range(pairs_per_block):
            v_scr[blk * pairs_per_block + t] = ref[:, t * LANE:(t + 1) * LANE]
    first_head_lanes = lax.broadcasted_iota(jnp.int32, (1, LANE), 1) < NA_HEAD_DIM
    first_head_rows = lax.broadcasted_iota(jnp.int32, (LANE, 1), 0) < NA_HEAD_DIM
    scale = NA_HEAD_DIM ** -0.5
    qt = NA_QUAD_ROWS * GRID_W
    nkeys = NA_KEY_ROWS * GRID_W

    def params(quad):
        r = g * NA_GROUP_ROWS + quad * NA_QUAD_ROWS
        w0 = jnp.clip(r - NA_WIN_ROWS // 2, 0, n_rows - NA_WIN_ROWS)
        rel = w0 - g * NA_GROUP_ROWS + NA_GROUP_ROWS
        koff = pl.multiple_of(rel * GRID_W, LANE)
        pr0 = lax.shift_right_logical(rel, 1)
        toff = pl.multiple_of((w0 - r + NA_WIN_ROWS - 1) * GRID_W, GRID_W)
        edge = jnp.logical_or(r == 0, r == n_rows - NA_QUAD_ROWS)
        return koff, pr0, toff, jnp.where(edge, 1, 0)

    def scores(quad):
        koff, _, toff, cls = params(quad)
        mask = mask_ref[cls]
        for p in range(NA_HEADS // 2):
            cs = slice(p * LANE, (p + 1) * LANE)
            qp = q_ref[quad * qt:(quad + 1) * qt, cs] * scale
            zero = jnp.zeros_like(qp)
            kw = k_scr[pl.ds(koff, nkeys), cs]
            for hh in range(2):
                qm = (jnp.where(first_head_lanes, qp, zero) if hh == 0
                      else jnp.where(first_head_lanes, zero, qp))
                s_scr[quad, 2 * p + hh] = _dot_nt(kw, qm) + tab_ref[2 * p + hh, pl.ds(toff, nkeys), :] + mask

    def softmax(quad):
        for h in range(NA_HEADS):
            s = s_scr[quad, h]
            e = jnp.exp(s - jnp.max(s, axis=0, keepdims=True))
            inv = 1.0 / jnp.sum(e, axis=0, keepdims=True)
            inv_scr[quad, h] = jnp.broadcast_to(inv, inv_scr.shape[2:])
            p_scr[quad, h] = e.astype(BF16)

    def values(quad):
        _, pr0, _, _ = params(quad)
        for p in range(NA_HEADS // 2):
            cs = slice(p * LANE, (p + 1) * LANE)
            vwin = jnp.concatenate([v_scr[pr0 + i, cs, :] for i in range(nkeys // LANE)], axis=1)
            outs = [_dot(vwin, p_scr[quad, 2 * p + hh]) * inv_scr[quad, 2 * p + hh, 0:1] for hh in range(2)]
            o_ref[cs, quad * qt:(quad + 1) * qt] = jnp.where(
                first_head_rows, outs[0], outs[1]).astype(o_ref.dtype)

    @pl.when(g >= 0)
    def _():
        scores(0)

    @pl.when(g >= 0)
    def _():
        softmax(0)
        scores(1)

    @pl.when(g >= 0)
    def _():
        values(0)
        softmax(1)

    @pl.when(g >= 0)
    def _():
        values(1)


def _na(qk, vt, tab, mask, *, batch, n_rows):
    n = qk.shape[0]
    gt = NA_GROUP_ROWS * GRID_W
    ng = n_rows // NA_GROUP_ROWS
    w = NA_WIDTH
    nkeys, nq = NA_KEY_ROWS * GRID_W, NA_QUAD_ROWS * GRID_W
    quads = NA_GROUP_ROWS // NA_QUAD_ROWS
    assert quads == 2, "the kernel body pipelines exactly two quads per group"

    def blk(g, shift):
        return jnp.clip(g + shift, 0, ng - 1)

    def kspec(shift):
        return pl.BlockSpec((gt, w), lambda b, g: (b * ng + blk(g, shift), 1))

    def vspec(shift):
        return pl.BlockSpec((w, gt), lambda b, g: (0, b * ng + blk(g, shift)))

    return pl.pallas_call(
        functools.partial(_na_kernel, n_rows=n_rows),
        grid=(batch, ng),
        in_specs=[pl.BlockSpec((gt, w), lambda b, g: (b * ng + g, 0)),
                  kspec(-1), kspec(0), kspec(1), vspec(-1), vspec(0), vspec(1),
                  pl.BlockSpec(tab.shape, lambda b, g: (0, 0, 0), pipeline_mode=pl.Buffered(1)),
                  pl.BlockSpec(mask.shape, lambda b, g: (0, 0, 0))],
        out_specs=pl.BlockSpec((w, gt), lambda b, g: (0, b * ng + g)),
        out_shape=jax.ShapeDtypeStruct((w, n), BF16),
        scratch_shapes=[pltpu.VMEM((3 * gt, w), BF16), pltpu.VMEM((3 * gt // LANE, w, LANE), BF16),
                        pltpu.VMEM((quads, NA_HEADS, nkeys, nq), F32),
                        pltpu.VMEM((quads, NA_HEADS, nkeys, nq), BF16),
                        pltpu.VMEM((quads, NA_HEADS, F32_ROWS, nq), F32)],
        compiler_params=_cparams(("parallel", "parallel")),
        name="na",
    )(qk, qk, qk, qk, vt, vt, vt, tab, mask)


def _hg_constants():
    c, sub = HG_CHUNK, HG_SUB
    t = np.arange(c)[:, None]
    s = np.arange(c)[None, :]
    nsub = c // sub
    i8 = np.arange(nsub)[:, None]
    mats = []
    for rev in (False, True):
        if not rev:
            inc = s <= t
            rb = s < sub * (t // sub)
            eb = s <= sub * (t // sub) + sub - 1
            r8 = s < sub * i8
        else:
            inc = s >= t
            rb = s > sub * (t // sub) + sub - 1
            eb = s >= sub * (t // sub)
            r8 = s > sub * i8 + sub - 1
        extra = np.zeros((BF16_ROWS * (nsub // BF16_ROWS + 1), c), bool)
        extra[:nsub] = r8
        extra[nsub] = True
        mats.append(np.concatenate([inc, rb, eb, extra], axis=0))
    tmat = np.stack(mats).astype(np.float32)
    lane_head = np.arange(HG_WIDTH) // HG_DIM
    blockdiag = (lane_head[:, None] == lane_head[None, :]).astype(np.float32)
    return tmat, blockdiag


def _hg_chunk(q, fpre, v, lb, st, tmat, blockones, rev):
    c, sub, nsub, w = HG_CHUNK, HG_SUB, HG_CHUNK // HG_SUB, HG_WIDTH
    f = lb + (1.0 - lb) * jax.nn.sigmoid(fpre)
    kk = 1.0 - f
    lf = jnp.log(f)
    lf_hi = lf.astype(BF16)
    lf_lo = (lf - lf_hi.astype(F32)).astype(BF16)
    cs = _dot(tmat, lf_hi) + _dot(tmat, lf_lo)
    b, rb, eb = cs[0:c], cs[c:2 * c], cs[2 * c:3 * c]
    r8 = cs[3 * c:3 * c + nsub]
    tot = cs[3 * c + nsub:3 * c + nsub + 1]

    lane = lax.broadcasted_iota(jnp.int32, (1, w), 1)
    head_masks = [(lane // HG_DIM) == h for h in range(HG_HEADS)]
    col_sub = (lane % HG_DIM) // sub

    qt = q * jnp.exp(b - rb)
    kt = kk * jnp.exp(eb - b)

    blocks, pairs = [], []
    for i in range(nsub):
        js = range(i) if not rev else range(i + 1, nsub)
        if len(js) == 0:
            continue
        mid = jnp.exp(r8[i:i + 1] - r8)
        for j in js:
            m = j + 1 if not rev else j - 1
            blocks.append(qt[sub * i:sub * (i + 1)] * mid[m:m + 1])
            pairs.append((i, j))
    qbig = jnp.concatenate(blocks, axis=0).astype(BF16)
    zero = jnp.zeros_like(kt)
    kstack = jnp.concatenate([jnp.where(hm, kt, zero) for hm in head_masks], axis=0).astype(BF16)
    roff = _dot_nt(qbig, kstack)
    rows = [jnp.zeros((sub, w), F32)] * nsub
    for n, (i, j) in enumerate(pairs):
        rows[i] = rows[i] + jnp.where(col_sub == j, roff[sub * n:sub * (n + 1)], 0.0)
    a_off = jnp.concatenate(rows, axis=0).astype(BF16)
    vstack = jnp.concatenate([jnp.where(hm, v, zero) for hm in head_masks], axis=0).astype(BF16)
    o = _dot(a_off, vstack)

    row_in_sub = lax.broadcasted_iota(jnp.int32, (nsub, sub, w), 1)
    b3, kk3, v3 = b.reshape(nsub, sub, w), kk.reshape(nsub, sub, w), v.reshape(nsub, sub, w)
    q3 = q.reshape(nsub, sub, w)
    dparts, vparts = [q3 * kk3], [v3]
    for d in range(1, sub):
        shift = d if not rev else sub - d
        valid = (row_in_sub >= d) if not rev else (row_in_sub + d < sub)
        bd = pltpu.roll(b3, shift, axis=1)
        kd = pltpu.roll(kk3, shift, axis=1)
        vparts.append(pltpu.roll(v3, shift, axis=1))
        ex = jnp.where(valid, b3 - bd, 0.0)
        dparts.append(jnp.where(valid, q3 * kd * jnp.exp(ex), 0.0))
    dstack = jnp.concatenate(dparts, axis=0).reshape(sub * c, w).astype(BF16)
    abc = _dot(dstack, blockones).reshape(sub, nsub, sub, w)
    for d in range(sub):
        o = o + (abc[d] * vparts[d]).reshape(c, w)

    o = o + _dot_nt((q * jnp.exp(b)).astype(BF16), st.astype(BF16))
    kdec = (kk * jnp.exp(tot - b)).astype(BF16)
    ut = _dot_tn(v.astype(BF16), kdec)
    st_new = st * jnp.exp(tot) + ut * blockones.astype(F32)
    return o, st_new


def _hg_kernel(qf_ref, ff_ref, vf_ref, qb_ref, fb_ref, vb_ref, lb_ref, tmat_ref, ones_ref,
               of_ref, ob_ref, sf_ref, sb_ref, *, chunks):
    @pl.when(pl.program_id(1) == 0)
    def _():
        sf_ref[...] = jnp.zeros_like(sf_ref)
        sb_ref[...] = jnp.zeros_like(sb_ref)

    blockones = ones_ref[...]
    lbf, lbb = lb_ref[0:1], lb_ref[1:2]

    def body(ci, carry):
        rf = pl.ds(pl.multiple_of(ci * HG_CHUNK, HG_CHUNK), HG_CHUNK)
        rb = pl.ds(pl.multiple_of((chunks - 1 - ci) * HG_CHUNK, HG_CHUNK), HG_CHUNK)
        o_f, s_f = _hg_chunk(qf_ref[rf], ff_ref[rf], vf_ref[rf], lbf, sf_ref[...], tmat_ref[0], blockones, False)
        of_ref[rf] = o_f
        sf_ref[...] = s_f
        o_b, s_b = _hg_chunk(qb_ref[rb], fb_ref[rb], vb_ref[rb], lbb, sb_ref[...], tmat_ref[1], blockones, True)
        ob_ref[rb] = o_b
        sb_ref[...] = s_b
        return carry

    lax.fori_loop(0, chunks, body, 0, unroll=2)


def _hgrn2(hg, lb2, *, batch, seq, chunks):
    n = hg.shape[0]
    tb = chunks * HG_CHUNK
    nb = seq // tb
    w = HG_WIDTH
    tmat, blockdiag = _hg_constants()
    tmat = jnp.asarray(tmat, BF16)
    blockones = jnp.asarray(blockdiag, BF16)

    def fwd(col):
        return pl.BlockSpec((tb, w), lambda b, i: (b * nb + i, col))

    def bwd(col):
        return pl.BlockSpec((tb, w), lambda b, i: (b * nb + nb - 1 - i, col))

    return pl.pallas_call(
        functools.partial(_hg_kernel, chunks=chunks),
        grid=(batch, nb),
        in_specs=[fwd(0), fwd(1), fwd(3), bwd(0), bwd(2), bwd(3),
                  pl.BlockSpec((2, w), lambda b, i: (0, 0)),
                  pl.BlockSpec(tmat.shape, lambda b, i: (0, 0, 0)),
                  pl.BlockSpec((w, w), lambda b, i: (0, 0))],
        out_specs=[fwd(0), bwd(0)],
        out_shape=[jax.ShapeDtypeStruct((n, w), F32), jax.ShapeDtypeStruct((n, w), F32)],
        scratch_shapes=[pltpu.VMEM((w, w), F32), pltpu.VMEM((w, w), F32)],
        compiler_params=_cparams(("parallel", "arbitrary")),
        name="hgrn2",
    )(hg, hg, hg, hg, hg, hg, lb2, tmat, blockones)


def _mem_kv_kernel(m_ref, g_ref, w_ref, o_ref):
    m = m_ref[...]
    h = (m * _rms_scale(m) * g_ref[...]).astype(BF16)
    o_ref[...] = _dot(h, w_ref[...]).astype(BF16)


def _mem_kv(mem2, g, w_bf, *, batch):
    n, d = mem2.shape
    m = n // batch
    return pl.pallas_call(
        _mem_kv_kernel,
        grid=(batch,),
        in_specs=[pl.BlockSpec((m, d), lambda b: (b, 0)),
                  pl.BlockSpec((1, d), lambda b: (0, 0)),
                  pl.BlockSpec(w_bf.shape, lambda b: (0, 0))],
        out_specs=pl.BlockSpec((m, w_bf.shape[1]), lambda b: (b, 0)),
        out_shape=jax.ShapeDtypeStruct((n, w_bf.shape[1]), BF16),
        compiler_params=_cparams(("parallel",)),
        name="mem_kv",
    )(mem2, g, w_bf)


def _mix_kernel(nat_ref, of_ref, ob_ref, gate_ref, mq_ref, kv_ref, x_ref, hgn_ref, ones_ref, wout_ref,
                fg_ref, wq_ref, sk_ref, x1_ref, h2t_ref, st_ref):
    w = HG_WIDTH
    lane = lax.broadcasted_iota(jnp.int32, (1, w), 1)
    head_masks = [(lane // HG_DIM) == h for h in range(HG_HEADS)]
    ones = ones_ref[...]

    o = of_ref[...] + ob_ref[...]
    o2 = o * o
    o2_hi = o2.astype(BF16)
    o2_lo = (o2 - o2_hi.astype(F32)).astype(BF16)
    ms = (_dot(o2_hi, ones) + _dot(o2_lo, ones)) * (1.0 / HG_DIM)
    g = gate_ref[...]
    hg_out = o * lax.rsqrt(ms + RMS_EPS) * hgn_ref[...] * (g * jax.nn.sigmoid(g))

    mq = mq_ref[...] * (MEM_WIDTH // MEM_HEADS) ** -0.5
    mk = kv_ref[:, :MEM_WIDTH]
    mv = kv_ref[:, MEM_WIDTH:]
    zero = jnp.zeros_like(mq)
    mem_out = jnp.zeros(mq.shape, F32)
    for hm in head_masks:
        s = _dot_nt(jnp.where(hm, mq, zero), mk)
        e = jnp.exp(s - jnp.max(s, axis=-1, keepdims=True))
        r = _dot(e.astype(BF16), mv) * (1.0 / jnp.sum(e, axis=-1, keepdims=True))
        mem_out = mem_out + jnp.where(hm, r, 0.0)

    nw = NA_WIDTH
    x1 = (x_ref[...] + _dot_tn(nat_ref[...], wout_ref[:nw])
          + _dot(hg_out.astype(BF16), wout_ref[nw:nw + w])
          + _dot(mem_out.astype(BF16), wout_ref[nw + w:]))
    x1_ref[...] = x1
    h2f = x1 * _rms_scale(x1) * fg_ref[...]
    h2t_ref[...] = h2f.T.astype(BF16)
    qp = _dot(h2f.astype(BF16), wq_ref[...]).astype(BF16)
    for hc in range(2 * PEER_HEADS):
        st_ref[hc] = _dot_nt(sk_ref[hc], qp[:, hc * PEER_NKEYS:(hc + 1) * PEER_NKEYS])


def _mix(na_t, o_f, o_b, hg, mq, kv, x2, hgn_g, wout_bf, ffn_g, wq_bf, sk_bf, *, seq, tm):
    n, d = x2.shape
    w = HG_WIDTH
    per_batch = seq // tm
    mem_tokens = kv.shape[0] // (n // seq)
    _, blockdiag = _hg_constants()
    blockones = jnp.asarray(blockdiag, BF16)
    nhc = 2 * PEER_HEADS

    def row(width, col=0):
        return pl.BlockSpec((tm, width), lambda i: (i, col))

    def full(a):
        return pl.BlockSpec(a.shape, lambda i: (0,) * a.ndim)

    return pl.pallas_call(
        _mix_kernel,
        grid=(n // tm,),
        in_specs=[pl.BlockSpec((NA_WIDTH, tm), lambda i: (0, i)), row(w), row(w), row(w, 4), row(MEM_WIDTH),
                  pl.BlockSpec((mem_tokens, kv.shape[1]), lambda i: (i // per_batch, 0)),
                  row(d), full(hgn_g), full(blockones), full(wout_bf), full(ffn_g), full(wq_bf), full(sk_bf)],
        out_specs=[row(d), pl.BlockSpec((d, tm), lambda i: (0, i)),
                   pl.BlockSpec((nhc, PEER_NKEYS, tm), lambda i: (0, 0, i))],
        out_shape=[jax.ShapeDtypeStruct((n, d), F32), jax.ShapeDtypeStruct((d, n), BF16),
                   jax.ShapeDtypeStruct((nhc, PEER_NKEYS, n), F32)],
        compiler_params=_cparams(("parallel",)),
        name="mix",
    )(na_t, o_f, o_b, hg, mq, kv, x2, hgn_g, blockones, wout_bf, ffn_g, wq_bf, sk_bf)


def _top_values(s, count):
    row = lax.broadcasted_iota(jnp.int32, s.shape, 0).astype(F32)
    cur = s
    vals = []
    for k in range(count):
        m = jnp.max(cur, axis=0, keepdims=True)
        first = jnp.min(jnp.where(cur >= m, row, float(s.shape[0])), axis=0, keepdims=True)
        vals.append(m)
        cur = jnp.where(row == first, -(k + 1) * KNOCK, cur)
    rank = jnp.where(cur <= -KNOCK, cur * (-1.0 / KNOCK) - 1.0, float(count))
    return vals, rank


def _top_values_no_ties(s, count):
    cur = s
    vals = []
    for k in range(count):
        m = jnp.max(cur, axis=0, keepdims=True)
        vals.append(m)
        cur = jnp.where(cur >= m, -(k + 1) * KNOCK, cur)
    removed = jnp.sum(jnp.where(cur <= -KNOCK, 1.0, 0.0), axis=0, keepdims=True)
    rank = jnp.where(cur <= -KNOCK, cur * (-1.0 / KNOCK) - 1.0, float(count - 1))
    return vals, jnp.minimum(rank, float(count - 1)), jnp.where(removed != float(count), 1.0, 0.0)


def _route_tables_exact(s1, s2):
    k = PEER_TOPK
    v1, r1 = _top_values(s1, k)
    v2, r2 = _top_values(s2, k)
    pairs = [(a, b) for a in range(k) for b in range(k // (a + 1))]
    cand = [v1[a] + v2[b] for a, b in pairs]
    pad = (-len(cand)) % F32_ROWS
    cand = jnp.concatenate(cand + [jnp.full_like(cand[0], NEG_BIG)] * pad, axis=0)
    best, cand_rank = _top_values(cand, k)
    z = jnp.zeros_like(best[0])
    for c in best:
        z = z + jnp.exp(c - best[0])
    taken = jnp.where(cand_rank < float(k), 1.0, 0.0)
    per_rank = [jnp.zeros_like(best[0]) for _ in range(k)]
    for i, (a, _) in enumerate(pairs):
        per_rank[a] = per_rank[a] + taken[i:i + 1]
    cnt = jnp.zeros(s1.shape, F32)
    for a in range(k):
        cnt = cnt + jnp.where(r1 == float(a), per_rank[a], 0.0)
    return r2, jnp.exp(s2 - v2[0]), cnt, jnp.exp(s1 - v1[0]) * (1.0 / z)


def _route_tables_no_ties(s1, s2):
    k = PEER_TOPK
    v1, _, tie1 = _top_values_no_ties(s1, k + 1)
    v2, r2, tie2 = _top_values_no_ties(s2, k + 1)
    cand = [v1[a] + v2[b] for a in range(k + 1) for b in range((k + 1) // (a + 1))]
    pad = (-len(cand)) % F32_ROWS
    cand = jnp.concatenate(cand + [jnp.full_like(cand[0], NEG_BIG)] * pad, axis=0)
    best, _, tie3 = _top_values_no_ties(cand, k + 1)
    tau = 0.5 * (best[k - 1] + best[k])
    close = best[k - 1] - best[k] <= NEAR_TIE_REL * (jnp.abs(v1[0]) + jnp.abs(v2[0])) + NEAR_TIE_ABS
    unsure = jnp.maximum(jnp.maximum(tie1, tie2), jnp.maximum(tie3, jnp.where(close, 1.0, 0.0)))
    z = jnp.zeros_like(best[0])
    for c in best[:k]:
        z = z + jnp.exp(c - best[0])
    cnt = jnp.zeros(s1.shape, F32)
    for b in range(k):
        cnt = cnt + jnp.where(s1 >= tau - v2[b], 1.0, 0.0)
    return (r2, jnp.exp(s2 - v2[0]), cnt, jnp.exp(s1 - v1[0]) * (1.0 / z)), unsure


def _route_kernel(st_ref, r2_ref, e2_ref, l1_ref, e1_ref):
    def store(h, tables):
        r2, e2, cnt, e1 = tables
        r2_ref[h] = r2.astype(BF16)
        e2_ref[h] = e2.astype(BF16)
        l1_ref[h] = cnt
        e1_ref[h] = e1

    redo = []
    for h in range(PEER_HEADS):
        tables, unsure = _route_tables_no_ties(st_ref[2 * h], st_ref[2 * h + 1])
        store(h, tables)
        redo.append(jnp.max(unsure) > 0.0)

    for h in range(PEER_HEADS):
        @pl.when(redo[h])
        def _(h=h):
            store(h, _route_tables_exact(st_ref[2 * h], st_ref[2 * h + 1]))


def _route(st, *, tt):
    nhc, nk, n = st.shape
    heads = nhc // 2
    spec = pl.BlockSpec((heads, nk, tt), lambda i: (0, 0, i))
    return pl.pallas_call(
        _route_kernel,
        grid=(n // tt,),
        in_specs=[pl.BlockSpec((nhc, nk, tt), lambda i: (0, 0, i))],
        out_specs=[spec, spec, spec, spec],
        out_shape=[jax.ShapeDtypeStruct((heads, nk, n), BF16), jax.ShapeDtypeStruct((heads, nk, n), BF16),
                   jax.ShapeDtypeStruct((heads, nk, n), F32), jax.ShapeDtypeStruct((heads, nk, n), F32)],
        compiler_params=_cparams(("parallel",)),
        name="route",
    )(st)


def _gated_activations(act, r2_ref, e2_ref, l1_ref, e1_ref, c):
    nk = PEER_NKEYS
    tn = act.shape[1]
    pk = BF16_ROWS
    tiles = (nk // pk, pk, tn)

    def row_tile(ref, h, a, scale):
        return jnp.broadcast_to(scale * ref[h, c, a:a + 1, :], (pk, tn)).astype(BF16)[None]

    out = []
    for a in range(act.shape[0] // nk):
        half_gate = None
        for h in range(PEER_HEADS):
            limit = row_tile(l1_ref, h, a, 1.0)
            weight = row_tile(e1_ref, h, a, 0.5)
            term = jnp.where(r2_ref[h].reshape(tiles) < limit, e2_ref[h].reshape(tiles),
                             jnp.zeros((), BF16)) * weight
            half_gate = term if half_gate is None else half_gate + term
        x = act[a * nk:(a + 1) * nk].astype(BF16)
        out.append(x * (1.0 + lax.erf(x * (2.0 ** -0.5))) * half_gate.reshape(nk, tn))
    return jnp.concatenate(out, axis=0)


def _peer_kernel(h2t_ref, u_ref, vt_ref, r2_ref, e2_ref, l1_ref, e1_ref, x1_ref, fg_ref, o_ref, *scratch, ec):
    j = pl.program_id(1)
    acc = scratch[-1]
    nch = (len(scratch) - 1) // 2
    a_bufs, w_bufs = scratch[:nch], scratch[nch:2 * nch]

    @pl.when(j == 0)
    def _():
        acc[...] = jnp.zeros_like(acc)

    def stage_a(c):
        a_bufs[c][...] = _dot(u_ref[c * ec:(c + 1) * ec], h2t_ref[...])

    def stage_b(c):
        w_bufs[c][...] = _gated_activations(a_bufs[c][...], r2_ref, e2_ref, l1_ref, e1_ref, c)

    def stage_c(c):
        acc[...] += _dot(vt_ref[:, c * ec:(c + 1) * ec], w_bufs[c][...])

    for t in range(nch + 2):
        @pl.when(j >= 0)
        def _(t=t):
            if 0 <= t - 2 < nch:
                stage_c(t - 2)
            if 0 <= t - 1 < nch:
                stage_b(t - 1)
            if t < nch:
                stage_a(t)

    @pl.when(j == pl.num_programs(1) - 1)
    def _():
        y = x1_ref[...] + acc[...].T
        o_ref[...] = y * _rms_scale(y) * fg_ref[...]


def _peer(h2t, u_bf, vt_bf, r2, e2, l1, e1, x1, final_g, *, tn, ec, per_step):
    n, d = x1.shape
    n_exp = u_bf.shape[0]
    groups = ec // PEER_NKEYS
    chunks = n_exp // ec
    heads, nk = PEER_HEADS, PEER_NKEYS
    l1 = l1.reshape(heads, chunks, groups, n)
    e1 = e1.reshape(heads, chunks, groups, n)
    tab = pl.BlockSpec((heads, nk, tn), lambda i, j: (0, 0, i))
    sel = pl.BlockSpec((heads, per_step, groups, tn), lambda i, j: (0, j, 0, i))
    return pl.pallas_call(
        functools.partial(_peer_kernel, ec=ec),
        grid=(n // tn, chunks // per_step),
        in_specs=[pl.BlockSpec((d, tn), lambda i, j: (0, i)),
                  pl.BlockSpec((per_step * ec, d), lambda i, j: (j, 0)),
                  pl.BlockSpec((d, per_step * ec), lambda i, j: (0, j)),
                  tab, tab, sel, sel,
                  pl.BlockSpec((tn, d), lambda i, j: (i, 0)),
                  pl.BlockSpec((1, d), lambda i, j: (0, 0))],
        out_specs=pl.BlockSpec((tn, d), lambda i, j: (i, 0)),
        out_shape=jax.ShapeDtypeStruct((n, d), F32),
        scratch_shapes=([pltpu.VMEM((ec, tn), F32)] * per_step + [pltpu.VMEM((ec, tn), BF16)] * per_step
                        + [pltpu.VMEM((d, tn), F32)]),
        compiler_params=_cparams(("parallel", "arbitrary")),
        name="peer",
    )(h2t, u_bf, vt_bf, r2, e2, l1, e1, x1, final_g)


def kernel(x, mem, attn_norm_g, w_in, na_rpb, hg_lower_bound, hg_out_norm_g, mem_norm_g, w_mem_kv, w_out,
           ffn_norm_g, peer_w_query, peer_sub_keys, peer_u, peer_v, final_norm_g):
    batch, seq, d = x.shape
    n = batch * seq
    assert w_in.shape[0] == 1, "single-layer trunk"
    assert seq % (NA_GROUP_ROWS * GRID_W) == 0 and seq % (HG_CHUNKS_PER_STEP * HG_CHUNK) == 0
    assert n % PEER_TOKENS == 0 and seq % MIX_TOKENS == 0 and n % IN_PROJ_TOKENS == 0
    x2 = x.reshape(n, d)
    lb_table = jnp.cumsum(jax.nn.softmax(hg_lower_bound.astype(F32), axis=1), axis=1)
    qk, vt, hg, mq = _in_proj(x2, attn_norm_g[0].reshape(1, d), w_in[0].astype(BF16), tm=IN_PROJ_TOKENS)
    na_tab, na_mask = _na_tables(na_rpb[0])
    na_t = _na(qk, vt, na_tab, na_mask, batch=batch, n_rows=seq // GRID_W)
    o_f, o_b = _hgrn2(hg, lb_table[:, 0], batch=batch, seq=seq, chunks=HG_CHUNKS_PER_STEP)
    kv = _mem_kv(mem.reshape(-1, d), mem_norm_g[0].reshape(1, d), w_mem_kv[0].astype(BF16), batch=batch)
    sk = peer_sub_keys[0].reshape(2 * PEER_HEADS, PEER_NKEYS, -1).astype(BF16)
    x1, h2t, st = _mix(na_t, o_f, o_b, hg, mq, kv, x2, hg_out_norm_g[0].reshape(1, -1), w_out[0].astype(BF16),
                      ffn_norm_g[0].reshape(1, d), peer_w_query[0].astype(BF16), sk, seq=seq, tm=MIX_TOKENS)
    r2, e2, l1, e1 = _route(st, tt=ROUTE_TOKENS)
    out = _peer(h2t, peer_u[0].astype(BF16), peer_v[0].T.astype(BF16), r2, e2, l1, e1, x1,
                final_norm_g.reshape(1, d), tn=PEER_TOKENS, ec=PEER_CHUNK, per_step=PEER_CHUNKS_PER_STEP)
    return out.reshape(batch, seq, d)
```
